```python
import jax, jax.numpy as jnp
from jax import lax
import numpy as np

D_MODEL = 1024
BATCH = 2
SEQ = 8192
DEPTH = 4
DEC_BATCH = 8
DEC_SEQ = 16
PAST_LEN = 4096

CHUNK = 64
N_A_LAYERS = DEPTH // 2
N_B_LAYERS = DEPTH - N_A_LAYERS
POOL_WINDOWS = (2, 4, 8, 16)
N_POOL_GROUPS = 4
POOL_GROUP_DIM = D_MODEL // N_POOL_GROUPS
POOL_HIST = max(POOL_WINDOWS) - 1
HEAD_DIM = 64
N_HEADS = D_MODEL // HEAD_DIM
N_KV_HEADS = 4
GQA_GROUP = N_HEADS // N_KV_HEADS
WINDOW = 128
WINDOW_CHUNKS = WINDOW // CHUNK
D_FF = -(-8 * D_MODEL // (3 * 256)) * 256
EPS = 1e-6
NEG_INF = -1e30

kernel_name = 'yoco_pool_swa_sink_stream_step'


def rms_norm(x, g):
    xf = x.astype(jnp.float32)
    y = xf * lax.rsqrt(jnp.mean(xf * xf, axis=-1, keepdims=True) + EPS)
    return (y * g.astype(jnp.float32)).astype(x.dtype)


def modulate(h, shift, scale):
    return h * (1 + scale[:, None, :]) + shift[:, None, :]


def pool_mixer(h, hist, pos0, w_pool, pool_scale):
    B, L, D = h.shape
    ext = jnp.concatenate([hist.astype(h.dtype), h], axis=1)
    csum = jnp.pad(jnp.cumsum(ext.astype(jnp.float32), axis=1), ((0, 0), (1, 0), (0, 0)))
    end = csum[:, POOL_HIST + 1:POOL_HIST + 1 + L, :]
    pos = pos0 + jnp.arange(L)
    parts = []
    for g, w in enumerate(POOL_WINDOWS):
        sl = slice(g * POOL_GROUP_DIM, (g + 1) * POOL_GROUP_DIM)
        start = csum[:, POOL_HIST + 1 - w:POOL_HIST + 1 - w + L, sl]
        count = jnp.minimum(w, pos + 1).astype(jnp.float32)[None, :, None]
        parts.append((end[..., sl] - start) / count)
    pooled = jnp.concatenate(parts, axis=-1)
    diff = (pooled - h.astype(jnp.float32)).astype(h.dtype).reshape(B, L, N_POOL_GROUPS, POOL_GROUP_DIM)
    out = jnp.einsum('blgc,gce->blge', diff, w_pool).reshape(B, L, D)
    return out * pool_scale


def window_attention(q, k_ext, v_ext, pos0, sinks):
    B, L = q.shape[0], q.shape[1]
    nb = -(-L // CHUNK)
    pad = nb * CHUNK - L
    qb = jnp.pad(q, ((0, 0), (0, pad), (0, 0), (0, 0))).reshape(B, nb, CHUNK, N_KV_HEADS, GQA_GROUP, HEAD_DIM)
    kblk = jnp.pad(k_ext, ((0, 0), (0, pad), (0, 0), (0, 0))).reshape(B, nb + WINDOW_CHUNKS, CHUNK, N_KV_HEADS, HEAD_DIM)
    vblk = jnp.pad(v_ext, ((0, 0), (0, pad), (0, 0), (0, 0))).reshape(B, nb + WINDOW_CHUNKS, CHUNK, N_KV_HEADS, HEAD_DIM)
    kb = jnp.concatenate([kblk[:, j:j + nb] for j in range(WINDOW_CHUNKS + 1)], axis=2)
    vb = jnp.concatenate([vblk[:, j:j + nb] for j in range(WINDOW_CHUNKS + 1)], axis=2)
    idx = jnp.arange((nb + WINDOW_CHUNKS) * CHUNK)
    valid = ((pos0 - WINDOW + idx) >= 0) & (idx < WINDOW + L)
    vmask = valid.reshape(nb + WINDOW_CHUNKS, CHUNK)
    mask = jnp.concatenate([vmask[j:j + nb] for j in range(WINDOW_CHUNKS + 1)], axis=1)
    logits = jnp.einsum('bnqkgd,bnskd->bnkgqs', qb.astype(jnp.float32), kb.astype(jnp.float32)) * (HEAD_DIM ** -0.5)
    logits = jnp.where(mask[None, :, None, None, None, :], logits, NEG_INF)
    sink = sinks.astype(jnp.float32).reshape(1, 1, N_KV_HEADS, GQA_GROUP, 1, 1)
    m = jnp.maximum(jnp.max(logits, axis=-1, keepdims=True), sink)
    p = jnp.exp(logits - m)
    denom = jnp.sum(p, axis=-1, keepdims=True) + jnp.exp(sink - m)
    out = jnp.einsum('bnkgqs,bnskd->bnqkgd', p / denom, vb.astype(jnp.float32))
    return out.reshape(B, nb * CHUNK, N_HEADS * HEAD_DIM)[:, :L].astype(q.dtype)


def swiglu(h, w_in, w_out):
    gate, up = jnp.split(h @ w_in, 2, axis=-1)
    return (jax.nn.silu(gate) * up) @ w_out


def trunk(x, c, pool_hist, k_hist, v_hist, pos0, w_ada, b_ada, g_mix, g_ffn, w_pool, pool_scale,
          w_q, g_q, sinks, w_o, g_kv, w_ada_kv, b_ada_kv, w_kv, g_k, w_ffn_in, w_ffn_out):
    B, L = x.shape[0], x.shape[1]
    c_act = jax.nn.silu(c)
    new_pool = []
    k_ext = None
    v_ext = None
    for i in range(DEPTH):
        ada = c_act @ w_ada[i] + b_ada[i]
        sh1, sc1, g1, sh2, sc2, g2 = jnp.split(ada, 6, axis=-1)
        h = modulate(rms_norm(x, g_mix[i]), sh1, sc1)
        if i < N_A_LAYERS:
            hist = pool_hist[i].astype(h.dtype)
            new_pool.append(jnp.concatenate([hist, h], axis=1)[:, -POOL_HIST:])
            mix = pool_mixer(h, hist, pos0, w_pool[i], pool_scale[i])
        else:
            j = i - N_A_LAYERS
            q = rms_norm((h @ w_q[j]).reshape(B, L, N_HEADS, HEAD_DIM), g_q[j])
            mix = window_attention(q, k_ext, v_ext, pos0, sinks[j]) @ w_o[j]
        x = x + g1[:, None, :] * mix
        h = modulate(rms_norm(x, g_ffn[i]), sh2, sc2)
        x = x + g2[:, None, :] * swiglu(h, w_ffn_in[i], w_ffn_out[i])
        if i == N_A_LAYERS - 1:
            sh_kv, sc_kv = jnp.split(c_act @ w_ada_kv + b_ada_kv, 2, axis=-1)
            hkv = modulate(rms_norm(x, g_kv), sh_kv, sc_kv)
            kv = (hkv @ w_kv).reshape(B, L, 2, N_KV_HEADS, HEAD_DIM)
            k_new = rms_norm(kv[:, :, 0], g_k)
            v_new = kv[:, :, 1]
            k_ext = jnp.concatenate([k_hist.astype(k_new.dtype), k_new], axis=1)
            v_ext = jnp.concatenate([v_hist.astype(v_new.dtype), v_new], axis=1)
    return x, jnp.stack(new_pool, axis=0), k_ext[:, -WINDOW:], v_ext[:, -WINDOW:]


def setup_inputs(seed: int = 0) -> dict:
    key = jax.random.key(seed)
    ks = jax.random.split(key, 32)
    nrm = jax.random.normal
    D = D_MODEL
    return {
        'x_prompt': nrm(ks[0], (BATCH, SEQ, D), jnp.float32),
        'x_sample': nrm(ks[1], (DEC_BATCH, DEC_SEQ, D), jnp.float32),
        'c_prompt': nrm(ks[2], (BATCH, D), jnp.float32),
        'c_sample': nrm(ks[3], (DEC_BATCH, D), jnp.float32),
        'state_pool': nrm(ks[4], (N_A_LAYERS, DEC_BATCH, POOL_HIST, D), jnp.float32),
        'cache_k': nrm(ks[5], (DEC_BATCH, WINDOW, N_KV_HEADS, HEAD_DIM), jnp.float32),
        'cache_v': nrm(ks[6], (DEC_BATCH, WINDOW, N_KV_HEADS, HEAD_DIM), jnp.float32),
        'w_ada': nrm(ks[7], (DEPTH, D, 6 * D), jnp.float32) * (0.3 * D ** -0.5),
        'b_ada': nrm(ks[8], (DEPTH, 6 * D), jnp.float32) * 0.02,
        'g_mix': 1.0 + 0.05 * nrm(ks[9], (DEPTH, D), jnp.float32),
        'g_ffn': 1.0 + 0.05 * nrm(ks[10], (DEPTH, D), jnp.float32),
        'w_pool': nrm(ks[11], (N_A_LAYERS, N_POOL_GROUPS, POOL_GROUP_DIM, POOL_GROUP_DIM), jnp.float32) * POOL_GROUP_DIM ** -0.5,
        'pool_scale': 1.0 + 0.1 * nrm(ks[12], (N_A_LAYERS, D), jnp.float32),
        'w_q': nrm(ks[13], (N_B_LAYERS, D, N_HEADS * HEAD_DIM), jnp.float32) * D ** -0.5,
        'g_q': 1.0 + 0.05 * nrm(ks[14], (N_B_LAYERS, HEAD_DIM), jnp.float32),
        'sinks': 0.5 * nrm(ks[15], (N_B_LAYERS, N_HEADS), jnp.float32),
        'w_o': nrm(ks[16], (N_B_LAYERS, N_HEADS * HEAD_DIM, D), jnp.float32) * (N_HEADS * HEAD_DIM) ** -0.5,
        'g_kv': 1.0 + 0.05 * nrm(ks[17], (D,), jnp.float32),
        'w_ada_kv': nrm(ks[18], (D, 2 * D), jnp.float32) * (0.3 * D ** -0.5),
        'b_ada_kv': nrm(ks[19], (2 * D,), jnp.float32) * 0.02,
        'w_kv': nrm(ks[20], (D, 2 * N_KV_HEADS * HEAD_DIM), jnp.float32) * D ** -0.5,
        'g_k': 1.0 + 0.05 * nrm(ks[21], (HEAD_DIM,), jnp.float32),
        'w_ffn_in': nrm(ks[22], (DEPTH, D, 2 * D_FF), jnp.float32) * D ** -0.5,
        'w_ffn_out': nrm(ks[23], (DEPTH, D_FF, D), jnp.float32) * D_FF ** -0.5,
    }


def reference(x_prompt, x_sample, c_prompt, c_sample, state_pool, cache_k, cache_v,
              w_ada, b_ada, g_mix, g_ffn, w_pool, pool_scale, w_q, g_q, sinks, w_o,
              g_kv, w_ada_kv, b_ada_kv, w_kv, g_k, w_ffn_in, w_ffn_out):
    B = x_prompt.shape[0]
    pool_zero = jnp.zeros((N_A_LAYERS, B, POOL_HIST, D_MODEL), x_prompt.dtype)
    kv_zero = jnp.zeros((B, WINDOW, N_KV_HEADS, HEAD_DIM), x_prompt.dtype)
    y_prompt, pool_p, k_p, v_p = trunk(x_prompt, c_prompt, pool_zero, kv_zero, kv_zero, 0,
                                       w_ada, b_ada, g_mix, g_ffn, w_pool, pool_scale, w_q, g_q, sinks, w_o,
                                       g_kv, w_ada_kv, b_ada_kv, w_kv, g_k, w_ffn_in, w_ffn_out)
    y_sample, pool_s, k_s, v_s = trunk(x_sample, c_sample, state_pool, cache_k, cache_v, PAST_LEN,
                                       w_ada, b_ada, g_mix, g_ffn, w_pool, pool_scale, w_q, g_q, sinks, w_o,
                                       g_kv, w_ada_kv, b_ada_kv, w_kv, g_k, w_ffn_in, w_ffn_out)
    return (y_prompt, y_sample, pool_p, k_p, v_p, pool_s, k_s, v_s)
```

```python
import functools

import jax
import jax.numpy as jnp
from jax import lax
from jax.experimental import pallas as pl
from jax.experimental.pallas import tpu as pltpu

D = 1024
DFF = 2816
HD = 64
NH = 16
NKV = 4
GQA = NH // NKV
KVD = NKV * HD
WIN = 128
CHUNK = 64
POOL_WINDOWS = (2, 4, 8, 16)
PGD = D // len(POOL_WINDOWS)
HALO = 16
EPS = 1e-6
NEG_INF = -1e30
BF = jnp.bfloat16
F32 = jnp.float32

V7X_VMEM_LIMIT_BYTES = 60 * 1024 * 1024
PROMPT_TILE_ROWS = 512
FFN_COL_CHUNK = 256
ADA_COL_BLOCKS = (1536, 1024, 128)


def _rms(x, g):
    return x * lax.rsqrt(jnp.mean(x * x, axis=-1, keepdims=True) + EPS) * g


def _sigmoid(x):
    return 1.0 / (1.0 + jnp.exp(-x))


def _expand(mod_ref, k, ns, r):
    if ns == 1:
        return mod_ref[0, k:k + 1, :]
    return jnp.concatenate([jnp.broadcast_to(mod_ref[s, k:k + 1, :], (r, D)) for s in range(ns)], axis=0)


def _rows(a, rows):
    return a if a.shape[0] == 1 else a[rows]


def _ffn(x1, sh2, sc2, g2, gffn_ref, win_ref, wout_ref, h2_ref, act_ref):
    h2_ref[...] = (_rms(x1, gffn_ref[...]) * (1.0 + sc2) + sh2).astype(BF)
    for c in range(DFF // FFN_COL_CHUNK):
        lo = c * FFN_COL_CHUNK
        hi = lo + FFN_COL_CHUNK
        gate = jnp.dot(h2_ref[...], win_ref[:, lo:hi], preferred_element_type=F32)
        up = jnp.dot(h2_ref[...], win_ref[:, DFF + lo:DFF + hi], preferred_element_type=F32)
        act_ref[:, lo:hi] = (gate * _sigmoid(gate) * up).astype(BF)
    y = jnp.dot(act_ref[...], wout_ref[...], preferred_element_type=F32)
    return x1 + g2 * y


def _pool_layer_kernel(*refs, ns, r, pos0, n_tiles, with_kv):
    if with_kv:
        (x_ref, mod_ref, hist_ref, gmix_ref, gffn_ref, wpool_ref, pscale_ref, win_ref, wout_ref,
         modkv_ref, gkv_ref, wkv_ref, gk_ref,
         o_ref, tail_ref, k_ref, v_ref, ext_ref, h2_ref, act_ref) = refs
    else:
        (x_ref, mod_ref, hist_ref, gmix_ref, gffn_ref, wpool_ref, pscale_ref, win_ref, wout_ref,
         o_ref, tail_ref, ext_ref, h2_ref, act_ref) = refs
    t = pl.program_id(1)
    tm = ns * r
    x = x_ref[...].reshape(tm, D)
    sh1, sc1, g1, sh2, sc2, g2 = [_expand(mod_ref, k, ns, r) for k in range(6)]
    h = _rms(x, gmix_ref[...]) * (1.0 + sc1) + sh1

    @pl.when(t == 0)
    def _():
        ext_ref[:, 0:HALO, :] = hist_ref[...]

    if n_tiles > 1:
        @pl.when(t > 0)
        def _():
            ext_ref[:, 0:HALO, :] = ext_ref[:, r:r + HALO, :]

    ext_ref[:, HALO:, :] = h.reshape(ns, r, D)
    tail_ref[...] = ext_ref[:, r:r + HALO, :]

    pos = pos0 + t * r + lax.broadcasted_iota(jnp.int32, (r, 1), 0)
    for s in range(ns):
        rows = slice(s * r, (s + 1) * r)
        for g, w in enumerate(POOL_WINDOWS):
            cols = slice(g * PGD, (g + 1) * PGD)
            e = ext_ref[s, :, cols]
            acc = e
            step = 1
            while step < w:
                acc = acc + pltpu.roll(acc, step, 0)
                step *= 2
            count = jnp.minimum(w, pos + 1).astype(F32)
            pooled = acc[HALO:, :] / count
            diff = (pooled - e[HALO:, :]).astype(BF)
            mix = jnp.dot(diff, wpool_ref[g], preferred_element_type=F32) * pscale_ref[:, cols]
            o_ref[s, :, cols] = x[rows, cols] + _rows(g1, rows)[:, cols] * mix

    x1 = o_ref[...].reshape(tm, D)
    x2 = _ffn(x1, sh2, sc2, g2, gffn_ref, win_ref, wout_ref, h2_ref, act_ref)
    o_ref[...] = x2.reshape(ns, r, D)

    if with_kv:
        shkv = _expand(modkv_ref, 0, ns, r)
        sckv = _expand(modkv_ref, 1, ns, r)
        hkv = (_rms(x2, gkv_ref[...]) * (1.0 + sckv) + shkv).astype(BF)
        kv = jnp.dot(hkv, wkv_ref[...], preferred_element_type=F32)
        for j in range(NKV):
            kj = _rms(kv[:, j * HD:(j + 1) * HD], gk_ref[...])
            k_ref[:, :, j * HD:(j + 1) * HD] = kj.reshape(ns, r, HD)
        v_ref[...] = kv[:, KVD:].reshape(ns, r, KVD)


def _attn_layer_kernel(x_ref, mod_ref, khist_ref, vhist_ref, knew_ref, vnew_ref, sinks_ref,
                       gmix_ref, gffn_ref, wq_ref, gq_ref, wo_ref, win_ref, wout_ref,
                       o_ref, kext_ref, vext_ref, qn_ref, attn_ref, h2_ref, act_ref,
                       *, ns, r, pos0, n_tiles, layer):
    t = pl.program_id(1)
    tm = ns * r
    x = x_ref[...].reshape(tm, D)
    sh1, sc1, g1, sh2, sc2, g2 = [_expand(mod_ref, k, ns, r) for k in range(6)]

    @pl.when(t == 0)
    def _():
        kext_ref[:, 0:WIN, :] = khist_ref[...].astype(BF)
        vext_ref[:, 0:WIN, :] = vhist_ref[...].astype(BF)

    if n_tiles > 1:
        @pl.when(t > 0)
        def _():
            kext_ref[:, 0:WIN, :] = kext_ref[:, r:r + WIN, :]
            vext_ref[:, 0:WIN, :] = vext_ref[:, r:r + WIN, :]

    kext_ref[:, WIN:, :] = knew_ref[...].astype(BF)
    vext_ref[:, WIN:, :] = vnew_ref[...].astype(BF)

    h = (_rms(x, gmix_ref[...]) * (1.0 + sc1) + sh1).astype(BF)
    q = jnp.dot(h, wq_ref[...], preferred_element_type=F32)
    for hd in range(NH):
        qn_ref[hd] = (_rms(q[:, hd * HD:(hd + 1) * HD], gq_ref[...]) * (HD ** -0.5)).astype(BF)

    ch = min(CHUNK, r)
    nk = WIN + ch
    thr = jnp.maximum(0, WIN - (pos0 + t * r))
    kidx = lax.broadcasted_iota(jnp.int32, (GQA * ch, nk), 1)

    for s in range(ns):
        def chunk_body(c, carry, s=s):
            r0 = pl.multiple_of(c * ch, ch)
            valid = (kidx + r0) >= thr
            for j in range(NKV):
                kvcols = slice(j * HD, (j + 1) * HD)
                qg = jnp.concatenate([qn_ref[j * GQA + g, pl.ds(s * r + r0, ch), :] for g in range(GQA)], axis=0)
                kc = kext_ref[s, pl.ds(r0, nk), kvcols]
                vc = vext_ref[s, pl.ds(r0, nk), kvcols]
                logits = lax.dot_general(qg, kc, (((1,), (1,)), ((), ())), preferred_element_type=F32)
                logits = jnp.where(valid, logits, NEG_INF)
                sink = jnp.concatenate(
                    [jnp.full((ch, 1), sinks_ref[layer, j * GQA + g], F32) for g in range(GQA)], axis=0)
                m = jnp.maximum(jnp.max(logits, axis=-1, keepdims=True), sink)
                p = jnp.exp(logits - m)
                denom = jnp.sum(p, axis=-1, keepdims=True) + jnp.exp(sink - m)
                o = jnp.dot(p.astype(BF), vc, preferred_element_type=F32) / denom
                for g in range(GQA):
                    hcols = slice((j * GQA + g) * HD, (j * GQA + g + 1) * HD)
                    attn_ref[pl.ds(s * r + r0, ch), hcols] = o[g * ch:(g + 1) * ch].astype(BF)
            return carry

        lax.fori_loop(0, r // ch, chunk_body, 0)

    mix = jnp.dot(attn_ref[...], wo_ref[...], preferred_element_type=F32)
    x1 = x + g1 * mix
    x2 = _ffn(x1, sh2, sc2, g2, gffn_ref, win_ref, wout_ref, h2_ref, act_ref)
    o_ref[...] = x2.reshape(ns, r, D)


def _ada_kernel(c_ref, w_ref, b_ref, o_ref):
    c = c_ref[...]
    c_act = (c * _sigmoid(c)).astype(BF)
    o_ref[0] = jnp.dot(c_act, w_ref[0].astype(BF), preferred_element_type=F32) + b_ref[0]


def _ada_call(c_all, w, b):
    nl, _, n = w.shape
    m = c_all.shape[0]
    nb = next(c for c in ADA_COL_BLOCKS if n % c == 0)
    return pl.pallas_call(
        _ada_kernel,
        grid=(nl, n // nb),
        in_specs=[
            pl.BlockSpec((m, D), lambda l, j: (0, 0)),
            pl.BlockSpec((1, D, nb), lambda l, j: (l, 0, j)),
            pl.BlockSpec((1, 1, nb), lambda l, j: (l, 0, j)),
        ],
        out_specs=pl.BlockSpec((1, m, nb), lambda l, j: (l, 0, j)),
        out_shape=jax.ShapeDtypeStruct((nl, m, n), F32),
        compiler_params=pltpu.CompilerParams(
            dimension_semantics=("arbitrary", "arbitrary"), vmem_limit_bytes=V7X_VMEM_LIMIT_BYTES),
        name="ada",
    )(c_all, w, b.reshape(nl, 1, n))


def _const(shape):
    return pl.BlockSpec(shape, lambda g, t: (0,) * len(shape), pipeline_mode=pl.Buffered(1))


def _tile(ns, r, width):
    return pl.BlockSpec((ns, r, width), lambda g, t: (g, t, 0))


def _per_group(ns, rows, width):
    return pl.BlockSpec((ns, rows, width), lambda g, t: (g, 0, 0))


def _layer_params():
    return pltpu.CompilerParams(
        dimension_semantics=("arbitrary", "arbitrary"), vmem_limit_bytes=V7X_VMEM_LIMIT_BYTES)


def _pool_layer_call(x, mod, hist, gmix, gffn, wpool, pscale, win, wout, kv_args, *, ns, r, pos0):
    bn, seq, _ = x.shape
    n_tiles = seq // r
    tm = ns * r
    with_kv = kv_args is not None
    in_specs = [
        _tile(ns, r, D), _per_group(ns, 6, D), _per_group(ns, HALO, D),
        _const((1, D)), _const((1, D)), _const((len(POOL_WINDOWS), PGD, PGD)), _const((1, D)),
        _const((D, 2 * DFF)), _const((DFF, D)),
    ]
    args = [x, mod, hist, gmix, gffn, wpool, pscale, win, wout]
    out_specs = [_tile(ns, r, D), _per_group(ns, HALO, D)]
    out_shape = [jax.ShapeDtypeStruct((bn, seq, D), F32), jax.ShapeDtypeStruct((bn, HALO, D), F32)]
    if with_kv:
        in_specs += [_per_group(ns, 2, D), _const((1, D)), _const((D, 2 * KVD)), _const((1, HD))]
        args += list(kv_args)
        out_specs += [_tile(ns, r, KVD), _tile(ns, r, KVD)]
        out_shape += [jax.ShapeDtypeStruct((bn, seq, KVD), F32)] * 2
    return pl.pallas_call(
        functools.partial(_pool_layer_kernel, ns=ns, r=r, pos0=pos0, n_tiles=n_tiles, with_kv=with_kv),
        grid=(bn // ns, n_tiles),
        in_specs=in_specs,
        out_specs=out_specs,
        out_shape=out_shape,
        scratch_shapes=[
            pltpu.VMEM((ns, r + HALO, D), F32),
            pltpu.VMEM((tm, D), BF),
            pltpu.VMEM((tm, DFF), BF),
        ],
        compiler_params=_layer_params(),
        name="pool_layer_kv" if with_kv else "pool_layer",
    )(*args)


def _attn_layer_call(x, mod, khist, vhist, knew, vnew, sinks, gmix, gffn, wq, gq, wo, win, wout,
                     *, ns, r, pos0, layer):
    bn, seq, _ = x.shape
    n_tiles = seq // r
    tm = ns * r
    return pl.pallas_call(
        functools.partial(_attn_layer_kernel, ns=ns, r=r, pos0=pos0, n_tiles=n_tiles, layer=layer),
        grid=(bn // ns, n_tiles),
        in_specs=[
            _tile(ns, r, D), _per_group(ns, 6, D), _per_group(ns, WIN, KVD), _per_group(ns, WIN, KVD),
            _tile(ns, r, KVD), _tile(ns, r, KVD),
            pl.BlockSpec(memory_space=pltpu.SMEM),
            _const((1, D)), _const((1, D)), _const((D, D)), _const((1, HD)), _const((D, D)),
            _const((D, 2 * DFF)), _const((DFF, D)),
        ],
        out_specs=_tile(ns, r, D),
        out_shape=jax.ShapeDtypeStruct((bn, seq, D), F32),
        scratch_shapes=[
            pltpu.VMEM((ns, r + WIN, KVD), BF),
            pltpu.VMEM((ns, r + WIN, KVD), BF),
            pltpu.VMEM((NH, tm, HD), BF),
            pltpu.VMEM((tm, D), BF),
            pltpu.VMEM((tm, D), BF),
            pltpu.VMEM((tm, DFF), BF),
        ],
        compiler_params=_layer_params(),
        name="attn_layer",
    )(x, mod, khist, vhist, knew, vnew, sinks, gmix, gffn, wq, gq, wo, win, wout)


def _trunk(x, mod, modkv, pool_hist, khist, vhist, wts, *, ns, r, pos0):
    n_a = pool_hist.shape[0]
    tails = []
    knew = vnew = None
    for i in range(n_a):
        kv_args = None
        if i == n_a - 1:
            kv_args = (modkv, wts["g_kv"], wts["w_kv"], wts["g_k"])
        outs = _pool_layer_call(x, mod[i], pool_hist[i], wts["g_mix"][i], wts["g_ffn"][i], wts["w_pool"][i],
                                wts["pool_scale"][i], wts["w_ffn_in"][i], wts["w_ffn_out"][i], kv_args,
                                ns=ns, r=r, pos0=pos0)
        x = outs[0]
        tails.append(outs[1][:, 1:, :])
        if kv_args is not None:
            knew, vnew = outs[2], outs[3]
    for j in range(wts["w_q"].shape[0]):
        i = n_a + j
        x = _attn_layer_call(x, mod[i], khist, vhist, knew, vnew, wts["sinks"], wts["g_mix"][i], wts["g_ffn"][i],
                             wts["w_q"][j], wts["g_q"][j], wts["w_o"][j], wts["w_ffn_in"][i], wts["w_ffn_out"][i],
                             ns=ns, r=r, pos0=pos0, layer=j)
    k_tail = jnp.concatenate([khist, knew], axis=1)[:, -WIN:].reshape(-1, WIN, NKV, HD)
    v_tail = jnp.concatenate([vhist, vnew], axis=1)[:, -WIN:].reshape(-1, WIN, NKV, HD)
    return x, jnp.stack(tails, axis=0), k_tail, v_tail


def kernel(x_prompt, x_sample, c_prompt, c_sample, state_pool, cache_k, cache_v, w_ada, b_ada, g_mix, g_ffn, w_pool, pool_scale, w_q, g_q, sinks, w_o, g_kv, w_ada_kv, b_ada_kv, w_kv, g_k, w_ffn_in, w_ffn_out):
    depth = w_ada.shape[0]
    n_a = state_pool.shape[0]
    bp = x_prompt.shape[0]
    bs, ls, _ = x_sample.shape
    past_len = 4096

    c_all = jnp.concatenate([c_prompt, c_sample, jnp.zeros((16 - bp - bs, D), F32)], axis=0)
    ada = _ada_call(c_all, w_ada, b_ada).reshape(depth, 16, 6, D)
    ada_kv = _ada_call(c_all, w_ada_kv[None], b_ada_kv[None]).reshape(16, 2, D)

    wts = dict(
        g_mix=g_mix.reshape(depth, 1, D), g_ffn=g_ffn.reshape(depth, 1, D),
        w_pool=w_pool.astype(BF), pool_scale=pool_scale.reshape(n_a, 1, D),
        w_q=w_q.astype(BF), g_q=g_q.reshape(-1, 1, HD), sinks=sinks, w_o=w_o.astype(BF),
        g_kv=g_kv.reshape(1, D), w_kv=w_kv.astype(BF), g_k=g_k.reshape(1, HD),
        w_ffn_in=w_ffn_in.astype(BF), w_ffn_out=w_ffn_out.astype(BF),
    )

    zero_pool = jnp.zeros((n_a, bp, HALO, D), F32)
    zero_kv = jnp.zeros((bp, WIN, KVD), F32)
    y_p, pool_p, k_p, v_p = _trunk(
        x_prompt, ada[:, :bp], ada_kv[:bp], zero_pool, zero_kv, zero_kv, wts,
        ns=1, r=PROMPT_TILE_ROWS, pos0=0)

    pool_hist_s = jnp.pad(state_pool, ((0, 0), (0, 0), (HALO - state_pool.shape[2], 0), (0, 0)))
    y_s, pool_s, k_s, v_s = _trunk(
        x_sample, ada[:, bp:bp + bs], ada_kv[bp:bp + bs], pool_hist_s,
        cache_k.reshape(bs, WIN, KVD), cache_v.reshape(bs, WIN, KVD), wts,
        ns=bs, r=ls, pos0=past_len)
    return (y_p, y_s, pool_p, k_p, v_p, pool_s, k_s, v_s)
```

```python
import functools

import jax
import jax.numpy as jnp
from jax import lax
from jax.experimental import pallas as pl
from jax.experimental.pallas import tpu as pltpu

D = 1024
DFF = 2816
HD = 64
NH = 16
NKV = 4
GQA = NH // NKV
KVD = NKV * HD
WIN = 128
CHUNK = 64
PAST_LEN = 4096
POOL_WINDOWS = (2, 4, 8, 16)
PGD = D // len(POOL_WINDOWS)
HALO = 16
LANES = 128
EPS = 1e-6
NEG_INF = -1e30
BF = jnp.bfloat16
F32 = jnp.float32

V7X_VMEM_LIMIT_BYTES = 60 * 1024 * 1024
PROMPT_TILE_ROWS = 512
FFN_COL_CHUNK = 256
ADA_COL_BLOCKS = (1536, 1024, 128)

NT_DIMS = (((1,), (1,)), ((), ()))
TN_DIMS = (((0,), (0,)), ((), ()))


def _rms(x, g):
    return x * lax.rsqrt(jnp.mean(x * x, axis=-1, keepdims=True) + EPS) * g


def _sigmoid(x):
    return 1.0 / (1.0 + jnp.exp(-x))


def _expand(mod_ref, k, ns, r):
    if ns == 1:
        return mod_ref[0, k:k + 1, :]
    return jnp.concatenate([jnp.broadcast_to(mod_ref[s, k:k + 1, :], (r, D)) for s in range(ns)], axis=0)


def _rows(a, rows):
    return a if a.shape[0] == 1 else a[rows]


def _ffn(x1, sh2, sc2, g2, gffn_ref, win_ref, wout_ref, h2_ref, act_ref):
    h2_ref[...] = (_rms(x1, gffn_ref[...]) * (1.0 + sc2) + sh2).astype(BF)
    for c in range(DFF // FFN_COL_CHUNK):
        lo = c * FFN_COL_CHUNK
        hi = lo + FFN_COL_CHUNK
        gate = jnp.dot(h2_ref[...], win_ref[:, lo:hi], preferred_element_type=F32)
        up = jnp.dot(h2_ref[...], win_ref[:, DFF + lo:DFF + hi], preferred_element_type=F32)
        act_ref[:, lo:hi] = (gate * _sigmoid(gate) * up).astype(BF)
    y = jnp.dot(act_ref[...], wout_ref[...], preferred_element_type=F32)
    return x1 + g2 * y


def _pool_layer_kernel(*refs, ns, r, pos0, n_tiles, with_kv):
    if with_kv:
        (x_ref, mod_ref, hist_ref, gmix_ref, gffn_ref, wpool_ref, pscale_ref, win_ref, wout_ref,
         modkv_ref, gkv_ref, wkv_ref, wvt_ref, gk_ref,
         o_ref, tail_ref, ktail_ref, vtail_ref, kbf_ref, vt_ref, ext_ref, h2_ref, act_ref) = refs
    else:
        (x_ref, mod_ref, hist_ref, gmix_ref, gffn_ref, wpool_ref, pscale_ref, win_ref, wout_ref,
         o_ref, tail_ref, ext_ref, h2_ref, act_ref) = refs
    t = pl.program_id(1)
    tm = ns * r
    x = x_ref[...].reshape(tm, D)
    sh1, sc1, g1, sh2, sc2, g2 = [_expand(mod_ref, k, ns, r) for k in range(6)]
    h = _rms(x, gmix_ref[...]) * (1.0 + sc1) + sh1

    @pl.when(t == 0)
    def _():
        ext_ref[:, 0:HALO, :] = hist_ref[...]

    if n_tiles > 1:
        @pl.when(t > 0)
        def _():
            ext_ref[:, 0:HALO, :] = ext_ref[:, r:r + HALO, :]

    ext_ref[:, HALO:, :] = h.reshape(ns, r, D)
    tail_ref[...] = ext_ref[:, r:r + HALO, :]

    pos = pos0 + t * r + lax.broadcasted_iota(jnp.int32, (r, 1), 0)
    for s in range(ns):
        rows = slice(s * r, (s + 1) * r)
        for g, w in enumerate(POOL_WINDOWS):
            cols = slice(g * PGD, (g + 1) * PGD)
            e = ext_ref[s, :, cols]
            acc = e
            step = 1
            while step < w:
                acc = acc + pltpu.roll(acc, step, 0)
                step *= 2
            count = jnp.minimum(w, pos + 1).astype(F32)
            pooled = acc[HALO:, :] / count
            diff = (pooled - e[HALO:, :]).astype(BF)
            mix = jnp.dot(diff, wpool_ref[g], preferred_element_type=F32) * pscale_ref[:, cols]
            o_ref[s, :, cols] = x[rows, cols] + _rows(g1, rows)[:, cols] * mix

    x1 = o_ref[...].reshape(tm, D)
    x2 = _ffn(x1, sh2, sc2, g2, gffn_ref, win_ref, wout_ref, h2_ref, act_ref)
    o_ref[...] = x2.reshape(ns, r, D)

    if with_kv:
        tr = ktail_ref.shape[1]
        shkv = _expand(modkv_ref, 0, ns, r)
        sckv = _expand(modkv_ref, 1, ns, r)
        hkv = (_rms(x2, gkv_ref[...]) * (1.0 + sckv) + shkv).astype(BF)
        kv = jnp.dot(hkv, wkv_ref[...], preferred_element_type=F32)
        for j in range(NKV):
            kj = _rms(kv[:, j * HD:(j + 1) * HD], gk_ref[...])
            kbf_ref[j] = kj.astype(BF)
            ktail_ref[:, :, j * HD:(j + 1) * HD] = kj.reshape(ns, r, HD)[:, r - tr:, :]
        vtail_ref[...] = kv[:, KVD:].reshape(ns, r, KVD)[:, r - tr:, :]
        vt_ref[...] = lax.dot_general(wvt_ref[...], hkv, NT_DIMS, preferred_element_type=F32).astype(BF)


def _attn_masks(ns, r, pos0, t, nk, k0, q0):
    kq = k0 + lax.broadcasted_iota(jnp.int32, (nk, LANES), 0)
    qq = q0 + lax.broadcasted_iota(jnp.int32, (nk, LANES), 1)
    if ns == 1:
        d = jnp.right_shift(kq, 6) - jnp.right_shift(qq, 6)
        first_valid = jnp.maximum(0, WIN - (pos0 + t * r))
        return (d >= 0) & (d <= WIN // CHUNK) & (kq >= first_valid)
    nh = ns * WIN
    is_hist = kq < nh
    key_seq = jnp.where(is_hist, jnp.right_shift(kq, 7), (kq - nh) // r)
    ok = key_seq == qq // r
    if pos0 < WIN:
        ok = ok & (~is_hist | ((kq & (WIN - 1)) >= WIN - pos0))
    return ok


def _attend(k_blk, vt_blk, qt_blks, mask, sinks):
    s = jnp.dot(k_blk, jnp.concatenate(qt_blks, axis=1), preferred_element_type=F32)
    ps, inv = [], []
    for g in range(GQA):
        sg = jnp.where(mask, s[:, g * LANES:(g + 1) * LANES], NEG_INF)
        sink = jnp.full((1, LANES), sinks[g], F32)
        m = jnp.maximum(jnp.max(sg, axis=0, keepdims=True), sink)
        p = jnp.exp(sg - m)
        inv.append(1.0 / (jnp.sum(p, axis=0, keepdims=True) + jnp.exp(sink - m)))
        ps.append(p.astype(BF))
    o = jnp.dot(vt_blk, jnp.concatenate(ps, axis=1), preferred_element_type=F32)
    return [o[:, g * LANES:(g + 1) * LANES] * inv[g] for g in range(GQA)]


def _attn_layer_kernel(x_ref, mod_ref, khist_ref, vthist_ref, knew_ref, vtnew_ref, sinks_ref,
                       gmix_ref, gffn_ref, wqt_ref, gq_ref, wo_ref, win_ref, wout_ref,
                       o_ref, kext_ref, vtext_ref, qt_ref, attnt_ref, h2_ref, act_ref,
                       *, ns, r, pos0, n_tiles, layer):
    t = pl.program_id(1)
    tm = ns * r
    nh = ns * WIN
    x = x_ref[...].reshape(tm, D)
    sh1, sc1, g1, sh2, sc2, g2 = [_expand(mod_ref, k, ns, r) for k in range(6)]

    @pl.when(t == 0)
    def _():
        kext_ref[:, 0:nh, :] = khist_ref[...].astype(BF)
        vtext_ref[:, 0:nh] = vthist_ref[...].astype(BF)

    if n_tiles > 1:
        @pl.when(t > 0)
        def _():
            kext_ref[:, 0:WIN, :] = kext_ref[:, r:r + WIN, :]
            vtext_ref[:, 0:WIN] = vtext_ref[:, r:r + WIN]

    kext_ref[:, nh:, :] = knew_ref[...]
    vtext_ref[:, nh:] = vtnew_ref[...]

    h = (_rms(x, gmix_ref[...]) * (1.0 + sc1) + sh1).astype(BF)
    qt = lax.dot_general(wqt_ref[...], h, NT_DIMS, preferred_element_type=F32)
    gq = jnp.concatenate([gq_ref[...]] * (tm // LANES), axis=1)
    for hd in range(NH):
        blk = qt[hd * HD:(hd + 1) * HD, :]
        inv = lax.rsqrt(jnp.mean(blk * blk, axis=0, keepdims=True) + EPS)
        qt_ref[hd * HD:(hd + 1) * HD, :] = (blk * inv * gq * (HD ** -0.5)).astype(BF)

    if ns == 1:
        blocks = [(q0, q0, WIN + LANES) for q0 in range(0, tm, LANES)]
    else:
        blocks = [(0, 0, nh + tm)]
    for q0, k0, nk in blocks:
        mask = _attn_masks(ns, r, pos0, t, nk, k0, q0)
        for j in range(NKV):
            heads = [j * GQA + g for g in range(GQA)]
            outs = _attend(
                kext_ref[j, k0:k0 + nk, :], vtext_ref[j * HD:(j + 1) * HD, k0:k0 + nk],
                [qt_ref[hd * HD:(hd + 1) * HD, q0:q0 + LANES] for hd in heads], mask,
                [sinks_ref[layer, hd] for hd in heads])
            for hd, o in zip(heads, outs):
                attnt_ref[hd * HD:(hd + 1) * HD, q0:q0 + LANES] = o.astype(BF)

    mix = lax.dot_general(attnt_ref[...], wo_ref[...], TN_DIMS, preferred_element_type=F32)
    x1 = x + g1 * mix
    x2 = _ffn(x1, sh2, sc2, g2, gffn_ref, win_ref, wout_ref, h2_ref, act_ref)
    o_ref[...] = x2.reshape(ns, r, D)


def _ada_kernel(c_ref, w_ref, b_ref, o_ref):
    c = c_ref[...]
    c_act = (c * _sigmoid(c)).astype(BF)
    o_ref[0] = jnp.dot(c_act, w_ref[0].astype(BF), preferred_element_type=F32) + b_ref[0]


def _ada_call(c_all, w, b):
    nl, _, n = w.shape
    m = c_all.shape[0]
    nb = next(c for c in ADA_COL_BLOCKS if n % c == 0)
    return pl.pallas_call(
        _ada_kernel,
        grid=(nl, n // nb),
        in_specs=[
            pl.BlockSpec((m, D), lambda l, j: (0, 0)),
            pl.BlockSpec((1, D, nb), lambda l, j: (l, 0, j)),
            pl.BlockSpec((1, 1, nb), lambda l, j: (l, 0, j)),
        ],
        out_specs=pl.BlockSpec((1, m, nb), lambda l, j: (l, 0, j)),
        out_shape=jax.ShapeDtypeStruct((nl, m, n), F32),
        compiler_params=pltpu.CompilerParams(
            dimension_semantics=("arbitrary", "arbitrary"), vmem_limit_bytes=V7X_VMEM_LIMIT_BYTES),
        name="ada",
    )(c_all, w, b.reshape(nl, 1, n))


def _const(shape):
    return pl.BlockSpec(shape, lambda g, t: (0,) * len(shape), pipeline_mode=pl.Buffered(1))


def _tile(ns, r, width):
    return pl.BlockSpec((ns, r, width), lambda g, t: (g, t, 0))


def _per_group(ns, rows, width):
    return pl.BlockSpec((ns, rows, width), lambda g, t: (g, 0, 0))


def _layer_params():
    return pltpu.CompilerParams(
        dimension_semantics=("arbitrary", "arbitrary"), vmem_limit_bytes=V7X_VMEM_LIMIT_BYTES)


def _pool_layer_call(x, mod, hist, gmix, gffn, wpool, pscale, win, wout, kv_args, *, ns, r, pos0):
    bn, seq, _ = x.shape
    n_tiles = seq // r
    tm = ns * r
    with_kv = kv_args is not None
    in_specs = [
        _tile(ns, r, D), _per_group(ns, 6, D), _per_group(ns, HALO, D),
        _const((1, D)), _const((1, D)), _const((len(POOL_WINDOWS), PGD, PGD)), _const((1, D)),
        _const((D, 2 * DFF)), _const((DFF, D)),
    ]
    args = [x, mod, hist, gmix, gffn, wpool, pscale, win, wout]
    out_specs = [_tile(ns, r, D), _per_group(ns, HALO, D)]
    out_shape = [jax.ShapeDtypeStruct((bn, seq, D), F32), jax.ShapeDtypeStruct((bn, HALO, D), F32)]
    if with_kv:
        tr = min(r, WIN)
        in_specs += [_per_group(ns, 2, D), _const((1, D)), _const((D, 2 * KVD)), _const((KVD, D)), _const((1, HD))]
        args += list(kv_args)
        out_specs += [
            _per_group(ns, tr, KVD), _per_group(ns, tr, KVD),
            pl.BlockSpec((NKV, tm, HD), lambda g, t: (0, g * n_tiles + t, 0)),
            pl.BlockSpec((KVD, tm), lambda g, t: (0, g * n_tiles + t)),
        ]
        out_shape += [
            jax.ShapeDtypeStruct((bn, tr, KVD), F32), jax.ShapeDtypeStruct((bn, tr, KVD), F32),
            jax.ShapeDtypeStruct((NKV, bn * seq, HD), BF), jax.ShapeDtypeStruct((KVD, bn * seq), BF),
        ]
    return pl.pallas_call(
        functools.partial(_pool_layer_kernel, ns=ns, r=r, pos0=pos0, n_tiles=n_tiles, with_kv=with_kv),
        grid=(bn // ns, n_tiles),
        in_specs=in_specs,
        out_specs=out_specs,
        out_shape=out_shape,
        scratch_shapes=[
            pltpu.VMEM((ns, r + HALO, D), F32),
            pltpu.VMEM((tm, D), BF),
            pltpu.VMEM((tm, DFF), BF),
        ],
        compiler_params=_layer_params(),
        name="pool_layer_kv" if with_kv else "pool_layer",
    )(*args)


def _attn_layer_call(x, mod, khist, vthist, knew, vtnew, sinks, gmix, gffn, wqt, gq, wo, win, wout,
                     *, ns, r, pos0, layer):
    bn, seq, _ = x.shape
    n_tiles = seq // r
    tm = ns * r
    nh = ns * WIN
    assert ns == 1 or (n_tiles == 1 and r <= CHUNK)
    return pl.pallas_call(
        functools.partial(_attn_layer_kernel, ns=ns, r=r, pos0=pos0, n_tiles=n_tiles, layer=layer),
        grid=(bn // ns, n_tiles),
        in_specs=[
            _tile(ns, r, D), _per_group(ns, 6, D),
            pl.BlockSpec((NKV, nh, HD), lambda g, t: (0, g, 0)),
            pl.BlockSpec((KVD, nh), lambda g, t: (0, g)),
            pl.BlockSpec((NKV, tm, HD), lambda g, t: (0, g * n_tiles + t, 0)),
            pl.BlockSpec((KVD, tm), lambda g, t: (0, g * n_tiles + t)),
            pl.BlockSpec(memory_space=pltpu.SMEM),
            _const((1, D)), _const((1, D)), _const((D, D)), _const((HD, LANES)), _const((D, D)),
            _const((D, 2 * DFF)), _const((DFF, D)),
        ],
        out_specs=_tile(ns, r, D),
        out_shape=jax.ShapeDtypeStruct((bn, seq, D), F32),
        scratch_shapes=[
            pltpu.VMEM((NKV, nh + tm, HD), BF),
            pltpu.VMEM((KVD, nh + tm), BF),
            pltpu.VMEM((D, tm), BF),
            pltpu.VMEM((D, tm), BF),
            pltpu.VMEM((tm, D), BF),
            pltpu.VMEM((tm, DFF), BF),
        ],
        compiler_params=_layer_params(),
        name="attn_layer",
    )(x, mod, khist, vthist, knew, vtnew, sinks, gmix, gffn, wqt, gq, wo, win, wout)


def _trunk(x, mod, modkv, pool_hist, k_hist, v_hist, wts, *, ns, r, pos0):
    n_a = pool_hist.shape[0]
    bn = x.shape[0]
    tails = []
    kv_outs = None
    for i in range(n_a):
        kv_args = None
        if i == n_a - 1:
            kv_args = (modkv, wts["g_kv"], wts["w_kv"], wts["w_vt"], wts["g_k"])
        outs = _pool_layer_call(x, mod[i], pool_hist[i], wts["g_mix"][i], wts["g_ffn"][i], wts["w_pool"][i],
                                wts["pool_scale"][i], wts["w_ffn_in"][i], wts["w_ffn_out"][i], kv_args,
                                ns=ns, r=r, pos0=pos0)
        x = outs[0]
        tails.append(outs[1][:, 1:, :])
        if kv_args is not None:
            kv_outs = outs[2:]
    ktail, vtail, kbf, vt = kv_outs
    khist = k_hist.transpose(2, 0, 1, 3).reshape(NKV, bn * WIN, HD)
    vthist = v_hist.transpose(2, 3, 0, 1).reshape(KVD, bn * WIN)
    for j in range(wts["w_qt"].shape[0]):
        i = n_a + j
        x = _attn_layer_call(x, mod[i], khist, vthist, kbf, vt, wts["sinks"], wts["g_mix"][i], wts["g_ffn"][i],
                             wts["w_qt"][j], wts["g_q"][j], wts["w_o"][j], wts["w_ffn_in"][i], wts["w_ffn_out"][i],
                             ns=ns, r=r, pos0=pos0, layer=j)
    k_tail = jnp.concatenate([k_hist.reshape(bn, WIN, KVD), ktail], axis=1)[:, -WIN:].reshape(bn, WIN, NKV, HD)
    v_tail = jnp.concatenate([v_hist.reshape(bn, WIN, KVD), vtail], axis=1)[:, -WIN:].reshape(bn, WIN, NKV, HD)
    return x, jnp.stack(tails, axis=0), k_tail, v_tail


def kernel(x_prompt, x_sample, c_prompt, c_sample, state_pool, cache_k, cache_v, w_ada, b_ada, g_mix, g_ffn, w_pool, pool_scale, w_q, g_q, sinks, w_o, g_kv, w_ada_kv, b_ada_kv, w_kv, g_k, w_ffn_in, w_ffn_out):
    depth = w_ada.shape[0]
    n_a = state_pool.shape[0]
    bp = x_prompt.shape[0]
    bs, ls, _ = x_sample.shape

    c_all = jnp.concatenate([c_prompt, c_sample, jnp.zeros((16 - bp - bs, D), F32)], axis=0)
    ada = _ada_call(c_all, w_ada, b_ada).reshape(depth, 16, 6, D)
    ada_kv = _ada_call(c_all, w_ada_kv[None], b_ada_kv[None]).reshape(16, 2, D)

    wts = dict(
        g_mix=g_mix.reshape(depth, 1, D), g_ffn=g_ffn.reshape(depth, 1, D),
        w_pool=w_pool.astype(BF), pool_scale=pool_scale.reshape(n_a, 1, D),
        w_qt=w_q.astype(BF).transpose(0, 2, 1),
        g_q=jnp.broadcast_to(g_q[:, :, None], g_q.shape + (LANES,)),
        sinks=sinks, w_o=w_o.astype(BF),
        g_kv=g_kv.reshape(1, D), w_kv=w_kv.astype(BF), w_vt=w_kv[:, KVD:].astype(BF).T, g_k=g_k.reshape(1, HD),
        w_ffn_in=w_ffn_in.astype(BF), w_ffn_out=w_ffn_out.astype(BF),
    )

    zero_pool = jnp.zeros((n_a, bp, HALO, D), F32)
    zero_kv = jnp.zeros((bp, WIN, NKV, HD), F32)
    y_p, pool_p, k_p, v_p = _trunk(
        x_prompt, ada[:, :bp], ada_kv[:bp], zero_pool, zero_kv, zero_kv, wts,
        ns=1, r=PROMPT_TILE_ROWS, pos0=0)

    pool_hist_s = jnp.pad(state_pool, ((0, 0), (0, 0), (HALO - state_pool.shape[2], 0), (0, 0)))
    y_s, pool_s, k_s, v_s = _trunk(
        x_sample, ada[:, bp:bp + bs], ada_kv[bp:bp + bs], pool_hist_s, cache_k, cache_v, wts,
        ns=bs, r=ls, pos0=PAST_LEN)
    return (y_p, y_s, pool_p, k_p, v_p, pool_s, k_s, v_s)
```

```python
import functools

import jax
import jax.numpy as jnp
from jax import lax
from jax.experimental import pallas as pl
from jax.experimental.pallas import tpu as pltpu

D = 1024
DFF = 2816
HD = 64
NH = 16
NKV = 4
GQA = NH // NKV
KVD = NKV * HD
WIN = 128
CHUNK = 64
PAST_LEN = 4096
POOL_WINDOWS = (2, 4, 8, 16)
PGD = D // len(POOL_WINDOWS)
HALO = 16
LANES = 128
EPS = 1e-6
NEG_INF = -1e30
BF = jnp.bfloat16
F32 = jnp.float32

V7X_VMEM_LIMIT_BYTES = 60 * 1024 * 1024
PROMPT_TILE_ROWS = 512
FFN_COL_CHUNK = 256
ROW_PIECE = 128
PT_SLOTS = 3
ADA_COL_BLOCKS = (1536, 1024, 128)

NT_DIMS = (((1,), (1,)), ((), ()))
TN_DIMS = (((0,), (0,)), ((), ()))


def _rms(x, g):
    return x * lax.rsqrt(jnp.mean(x * x, axis=-1, keepdims=True) + EPS) * g


def _sigmoid(x):
    return 1.0 / (1.0 + jnp.exp(-x))


def _mod(mod_ref, k, ns, r):
    if ns == 1:
        return mod_ref[0, k:k + 1, :]
    return jnp.concatenate([jnp.broadcast_to(mod_ref[s, k:k + 1, :], (r, D)) for s in range(ns)], axis=0)


def _row_blocks(ns, r):
    tm = ns * r
    rb = min(tm, ROW_PIECE)
    assert ns == 1 or rb == tm
    return rb, [slice(b * rb, (b + 1) * rb) for b in range(tm // rb)]


def _load_rows(ref, rows, ns, r):
    return ref[0, rows, :] if ns == 1 else ref[...].reshape(ns * r, ref.shape[-1])


def _store_rows(ref, rows, val, ns, r):
    if ns == 1:
        ref[0, rows, :] = val
    else:
        ref[...] = val.reshape(ns, r, ref.shape[-1])


def _run(pieces):
    for piece in pieces:
        piece()


def _interleave(main, side):
    merged = [((i + 0.5) / len(main), 0, f) for i, f in enumerate(main)]
    merged += [((i + 0.5) / len(side), 1, f) for i, f in enumerate(side)]
    _run(f for _, _, f in sorted(merged, key=lambda e: e[:2]))


def _skewed_steps(n, first, odd, even):
    pl.when(n == 0)(first)
    pl.when(lax.rem(n, 2) == 1)(odd)
    pl.when((n > 0) & (lax.rem(n, 2) == 0))(even)


def _ffn_pieces(x1_ref, h2_ref, modc_ref, win_ref, wout_ref, act_ref, o_ref, *, ns, r):
    _, blocks = _row_blocks(ns, r)

    def chunk(c):
        lo = c * FFN_COL_CHUNK
        hi = lo + FFN_COL_CHUNK
        gate = jnp.dot(h2_ref[...], win_ref[0, :, lo:hi], preferred_element_type=F32)
        up = jnp.dot(h2_ref[...], win_ref[0, :, DFF + lo:DFF + hi], preferred_element_type=F32)
        act_ref[:, lo:hi] = (gate * _sigmoid(gate) * up).astype(BF)

    def rows_out(rows):
        y = jnp.dot(act_ref[rows, :], wout_ref[0], preferred_element_type=F32)
        g2 = _mod(modc_ref, 5, ns, r)
        _store_rows(o_ref, rows, x1_ref[rows, :] + g2 * y, ns, r)

    return ([functools.partial(chunk, c) for c in range(DFF // FFN_COL_CHUNK)]
            + [functools.partial(rows_out, rows) for rows in blocks])


def _lagged(firsts, seconds, lag):
    out = []
    for k in range(len(firsts) + lag):
        if k < len(firsts):
            out.append(firsts[k])
        if k >= lag:
            out.append(seconds[k - lag])
    return out


def _pool_mixer_pieces(x_ref, mod_ref, hist_ref, gmix_ref, gffn_ref, wpool_ref, pscale_ref,
                       tail_ref, ext_ref, diff_ref, x1_ref, h2_ref, t, *, ns, r, pos0, n_tiles):
    rp = min(r, ROW_PIECE)

    def pool(s, p):
        lo = p * rp
        rows = slice(s * r + lo, s * r + lo + rp)

        def m(k):
            return mod_ref[s, k:k + 1, :]

        if p == 0:
            halo = hist_ref[s]
            if n_tiles > 1:
                halo = jnp.where(t == 0, halo, ext_ref[s, r:r + HALO, :])
            ext_ref[s, 0:HALO, :] = halo
        x = x_ref[s, lo:lo + rp, :]
        h = _rms(x, gmix_ref[0]) * (1.0 + m(1)) + m(0)
        ext_ref[s, HALO + lo:HALO + lo + rp, :] = h
        if lo + rp == r:
            tail_ref[s] = h[rp - HALO:, :]

        pos = pos0 + t * r + lo + lax.broadcasted_iota(jnp.int32, (rp, 1), 0)
        for g, w in enumerate(POOL_WINDOWS):
            cols = slice(g * PGD, (g + 1) * PGD)
            e = ext_ref[s, lo:lo + rp + HALO, cols]
            acc = e
            step = 1
            while step < w:
                acc = acc + pltpu.roll(acc, step, 0)
                step *= 2
            count = jnp.minimum(w, pos + 1).astype(F32)
            pooled = acc[HALO:, :] / count
            diff_ref[rows, cols] = (pooled - e[HALO:, :]).astype(BF)

    def mix(s, p):
        lo = p * rp
        rows = slice(s * r + lo, s * r + lo + rp)

        def m(k):
            return mod_ref[s, k:k + 1, :]

        mixed = jnp.concatenate(
            [jnp.dot(diff_ref[rows, g * PGD:(g + 1) * PGD], wpool_ref[0, g], preferred_element_type=F32)
             for g in range(len(POOL_WINDOWS))], axis=1)
        x1 = x_ref[s, lo:lo + rp, :] + m(2) * (mixed * pscale_ref[0])
        x1_ref[rows, :] = x1
        h2_ref[rows, :] = (_rms(x1, gffn_ref[0]) * (1.0 + m(4)) + m(3)).astype(BF)

    sp = [(s, p) for s in range(ns) for p in range(r // rp)]
    return _lagged([functools.partial(pool, *e) for e in sp], [functools.partial(mix, *e) for e in sp], 1)


def _pool_layer_kernel(x_ref, mod_ref, modc_ref, hist_ref, gmix_ref, gffn_ref, wpool_ref, pscale_ref,
                       win_ref, wout_ref, o_ref, tail_ref,
                       ext_ref, diff_ref, x1a_ref, x1b_ref, h2a_ref, h2b_ref, act_ref,
                       *, ns, r, pos0, n_tiles, n_steps):
    n = pl.program_id(0)
    t = lax.rem(jnp.minimum(n, n_steps - 1), n_tiles)

    def mixer(x1_ref, h2_ref):
        return _pool_mixer_pieces(x_ref, mod_ref, hist_ref, gmix_ref, gffn_ref, wpool_ref, pscale_ref,
                                  tail_ref, ext_ref, diff_ref, x1_ref, h2_ref, t,
                                  ns=ns, r=r, pos0=pos0, n_tiles=n_tiles)

    def step(x1_cur, h2_cur, x1_nxt, h2_nxt):
        _interleave(_ffn_pieces(x1_cur, h2_cur, modc_ref, win_ref, wout_ref, act_ref, o_ref, ns=ns, r=r),
                    mixer(x1_nxt, h2_nxt))

    _skewed_steps(n, lambda: _run(mixer(x1a_ref, h2a_ref)),
                  lambda: step(x1a_ref, h2a_ref, x1b_ref, h2b_ref),
                  lambda: step(x1b_ref, h2b_ref, x1a_ref, h2a_ref))


def _attn_masks(ns, r, pos0, t, nk, k0, q0):
    kq = k0 + lax.broadcasted_iota(jnp.int32, (nk, LANES), 0)
    qq = q0 + lax.broadcasted_iota(jnp.int32, (nk, LANES), 1)
    if ns == 1:
        d = kq // CHUNK - qq // CHUNK
        first_valid = jnp.maximum(0, WIN - (pos0 + t * r))
        return (d >= 0) & (d <= WIN // CHUNK) & (kq >= first_valid)
    nh = ns * WIN
    is_hist = kq < nh
    key_seq = jnp.where(is_hist, kq // WIN, (kq - nh) // r)
    ok = key_seq == qq // r
    if pos0 < WIN:
        ok = ok & (~is_hist | (kq % WIN >= WIN - pos0))
    return ok


def _softmax_t(k_blk, qt_blks, mask, sinks):
    s = jnp.dot(k_blk, jnp.concatenate(qt_blks, axis=1), preferred_element_type=F32)
    ps, inv = [], []
    for g in range(GQA):
        sg = jnp.where(mask, s[:, g * LANES:(g + 1) * LANES], NEG_INF)
        sink = jnp.full((1, LANES), sinks[g], F32)
        m = jnp.maximum(jnp.max(sg, axis=0, keepdims=True), sink)
        p = jnp.exp(sg - m)
        inv.append(1.0 / (jnp.sum(p, axis=0, keepdims=True) + jnp.exp(sink - m)))
        ps.append(p.astype(BF))
    return jnp.concatenate(ps, axis=1), inv


def _attn_mixer_pieces(a, x1_ref, h2_ref, t, *, ns, r, pos0, n_tiles, layer, make_kv):
    tm = ns * r
    nh = ns * WIN
    _, blocks = _row_blocks(ns, r)
    x_ref, mod_ref, kext_ref, vtext_ref = a["x"], a["mod"], a["kext"], a["vtext"]
    hb_ref, qt_ref, attnt_ref = a["hb"], a["qt"], a["attnt"]

    def setup():
        kprev = a["khist"][...].astype(BF)
        vtprev = a["vthist"][...].astype(BF)
        if n_tiles > 1:
            kprev = jnp.where(t == 0, kprev, kext_ref[:, r:r + WIN, :])
            vtprev = jnp.where(t == 0, vtprev, vtext_ref[:, r:r + WIN])
        kext_ref[:, 0:nh, :] = kprev
        vtext_ref[:, 0:nh] = vtprev
        if not make_kv:
            kext_ref[:, nh:, :] = a["knew"][...]
            vtext_ref[:, nh:] = a["vtnew"][...]

    def rows_in(rows):
        x = _load_rows(x_ref, rows, ns, r)
        xn = x * lax.rsqrt(jnp.mean(x * x, axis=-1, keepdims=True) + EPS)
        hb_ref[rows, :] = (xn * a["gmix"][0] * (1.0 + _mod(mod_ref, 1, ns, r)) + _mod(mod_ref, 0, ns, r)).astype(BF)
        if make_kv:
            modkv_ref = a["modkv"]
            a["hkv"][rows, :] = (
                xn * a["gkv"][...] * (1.0 + _mod(modkv_ref, 1, ns, r)) + _mod(modkv_ref, 0, ns, r)).astype(BF)

    def rows_kv(rows):
        kv = jnp.dot(a["hkv"][rows, :], a["wkv"][...], preferred_element_type=F32)
        v = kv[:, KVD:]
        ks = [_rms(kv[:, j * HD:(j + 1) * HD], a["gk"][...]) for j in range(NKV)]
        ext_rows = slice(nh + rows.start, nh + rows.stop)
        for j in range(NKV):
            kj = ks[j].astype(BF)
            kext_ref[j, ext_rows, :] = kj
            a["kbf"][j, rows, :] = kj
        vt = v.T.astype(BF)
        vtext_ref[:, ext_rows] = vt
        a["vt"][:, rows] = vt
        tr = a["ktail"].shape[1]
        if ns > 1:
            for j in range(NKV):
                a["ktail"][:, :, j * HD:(j + 1) * HD] = ks[j].reshape(ns, r, HD)[:, r - tr:, :]
            a["vtail"][...] = v.reshape(ns, r, KVD)[:, r - tr:, :]
        elif rows.stop == r:
            for j in range(NKV):
                a["ktail"][0, :, j * HD:(j + 1) * HD] = ks[j]
            a["vtail"][0] = v

    def q_heads(j):
        qt = lax.dot_general(a["wqt"][0, j * GQA * HD:(j + 1) * GQA * HD, :], hb_ref[...], NT_DIMS,
                             preferred_element_type=F32)
        gq = jnp.concatenate([a["gq"][0]] * (tm // LANES), axis=1)
        for g in range(GQA):
            blk = qt[g * HD:(g + 1) * HD, :]
            inv = lax.rsqrt(jnp.mean(blk * blk, axis=0, keepdims=True) + EPS)
            hd = j * GQA + g
            qt_ref[hd * HD:(hd + 1) * HD, :] = (blk * inv * gq * (HD ** -0.5)).astype(BF)

    inv_denoms = {}

    def scores(k, j, q0, k0, nk):
        mask = _attn_masks(ns, r, pos0, t, nk, k0, q0)
        heads = [j * GQA + g for g in range(GQA)]
        pt, inv_denoms[k] = _softmax_t(
            kext_ref[j, k0:k0 + nk, :], [qt_ref[hd * HD:(hd + 1) * HD, q0:q0 + LANES] for hd in heads], mask,
            [a["sinks"][layer, hd] for hd in heads])
        a["pt"][k % PT_SLOTS] = pt

    def values(k, j, q0, k0, nk):
        o = jnp.dot(vtext_ref[j * HD:(j + 1) * HD, k0:k0 + nk], a["pt"][k % PT_SLOTS],
                    preferred_element_type=F32)
        for g, inv in enumerate(inv_denoms.pop(k)):
            hd = j * GQA + g
            attnt_ref[hd * HD:(hd + 1) * HD, q0:q0 + LANES] = (o[:, g * LANES:(g + 1) * LANES] * inv).astype(BF)

    def rows_mix(rows):
        mix = lax.dot_general(attnt_ref[:, rows], a["wo"][0], TN_DIMS, preferred_element_type=F32)
        x1 = _load_rows(x_ref, rows, ns, r) + _mod(mod_ref, 2, ns, r) * mix
        x1_ref[rows, :] = x1
        h2_ref[rows, :] = (_rms(x1, a["gffn"][0]) * (1.0 + _mod(mod_ref, 4, ns, r)) + _mod(mod_ref, 3, ns, r)).astype(BF)

    if ns == 1:
        windows = [(q0, q0, WIN + LANES) for q0 in range(0, tm, LANES)]
    else:
        windows = [(0, 0, nh + tm)]
    pieces = [setup] + [functools.partial(rows_in, rows) for rows in blocks]
    if make_kv:
        pieces += [functools.partial(rows_kv, rows) for rows in blocks]
    pieces += [functools.partial(q_heads, j) for j in range(NKV)]
    att = [(k, j) + w for k, (j, w) in enumerate((j, w) for j in range(NKV) for w in windows)]
    pieces += _lagged([functools.partial(scores, *e) for e in att], [functools.partial(values, *e) for e in att],
                      PT_SLOTS - 1)
    return pieces + [functools.partial(rows_mix, rows) for rows in blocks]


def _attn_layer_kernel(*refs, names, ns, r, pos0, n_tiles, n_steps, layer, make_kv):
    a = dict(zip(names, refs))
    n = pl.program_id(0)
    t = lax.rem(jnp.minimum(n, n_steps - 1), n_tiles)

    def mixer(x1_ref, h2_ref):
        return _attn_mixer_pieces(a, x1_ref, h2_ref, t, ns=ns, r=r, pos0=pos0, n_tiles=n_tiles,
                                  layer=layer, make_kv=make_kv)

    def step(x1_cur, h2_cur, x1_nxt, h2_nxt):
        _interleave(_ffn_pieces(x1_cur, h2_cur, a["modc"], a["win"], a["wout"], a["act"], a["o"], ns=ns, r=r),
                    mixer(x1_nxt, h2_nxt))

    _skewed_steps(n, lambda: _run(mixer(a["x1a"], a["h2a"])),
                  lambda: step(a["x1a"], a["h2a"], a["x1b"], a["h2b"]),
                  lambda: step(a["x1b"], a["h2b"], a["x1a"], a["h2a"]))


def _ada_kernel(c_ref, w_ref, b_ref, o_ref):
    c = c_ref[...]
    c_act = (c * _sigmoid(c)).astype(BF)
    o_ref[0] = jnp.dot(c_act, w_ref[0].astype(BF), preferred_element_type=F32) + b_ref[0]


def _ada_call(c_all, w, b):
    nl, _, n = w.shape
    m = c_all.shape[0]
    nb = next(c for c in ADA_COL_BLOCKS if n % c == 0)
    return pl.pallas_call(
        _ada_kernel,
        grid=(nl, n // nb),
        in_specs=[
            pl.BlockSpec((m, D), lambda l, j: (0, 0)),
            pl.BlockSpec((1, D, nb), lambda l, j: (l, 0, j)),
            pl.BlockSpec((1, 1, nb), lambda l, j: (l, 0, j)),
        ],
        out_specs=pl.BlockSpec((1, m, nb), lambda l, j: (l, 0, j)),
        out_shape=jax.ShapeDtypeStruct((nl, m, n), F32),
        compiler_params=pltpu.CompilerParams(
            dimension_semantics=("arbitrary", "arbitrary"), vmem_limit_bytes=V7X_VMEM_LIMIT_BYTES),
        name="ada",
    )(c_all, w, b.reshape(nl, 1, n))


class _Specs:
    def __init__(self, ns, r, n_tiles, n_steps):
        self.ns, self.r, self.n_tiles = ns, r, n_tiles
        self.mix = lambda n: jnp.minimum(n, n_steps - 1)
        self.ffn = lambda n: jnp.maximum(n - 1, 0)

    def tile(self, which, width):
        return pl.BlockSpec((self.ns, self.r, width),
                            lambda n: (which(n) // self.n_tiles, which(n) % self.n_tiles, 0))

    def group(self, which, rows, width):
        return pl.BlockSpec((self.ns, rows, width), lambda n: (which(n) // self.n_tiles, 0, 0))

    def k_rows(self, which, rows_per_group=None):
        if rows_per_group is None:
            return pl.BlockSpec((NKV, self.ns * self.r, HD), lambda n: (0, which(n), 0))
        return pl.BlockSpec((NKV, self.ns * rows_per_group, HD), lambda n: (0, which(n) // self.n_tiles, 0))

    def vt_cols(self, which, rows_per_group=None):
        if rows_per_group is None:
            return pl.BlockSpec((KVD, self.ns * self.r), lambda n: (0, which(n)))
        return pl.BlockSpec((KVD, self.ns * rows_per_group), lambda n: (0, which(n) // self.n_tiles))

    @staticmethod
    def const(shape):
        return pl.BlockSpec(shape, lambda n: (0,) * len(shape), pipeline_mode=pl.Buffered(1))

    @staticmethod
    def layer(shape, i):
        return pl.BlockSpec((1,) + shape, lambda n: (i,) + (0,) * len(shape), pipeline_mode=pl.Buffered(1))


def _layer_params():
    return pltpu.CompilerParams(dimension_semantics=("arbitrary",), vmem_limit_bytes=V7X_VMEM_LIMIT_BYTES)


def _pool_layer_call(x, mod, hist, wts, i, *, ns, r, pos0):
    bn, seq, _ = x.shape
    n_tiles = seq // r
    n_steps = (bn // ns) * n_tiles
    tm = ns * r
    sp = _Specs(ns, r, n_tiles, n_steps)
    return pl.pallas_call(
        functools.partial(_pool_layer_kernel, ns=ns, r=r, pos0=pos0, n_tiles=n_tiles, n_steps=n_steps),
        grid=(n_steps + 1,),
        in_specs=[
            sp.tile(sp.mix, D), sp.group(sp.mix, 6, D), sp.group(sp.ffn, 6, D), sp.group(sp.mix, HALO, D),
            sp.layer((1, D), i), sp.layer((1, D), i), sp.layer((len(POOL_WINDOWS), PGD, PGD), i),
            sp.layer((1, D), i), sp.layer((D, 2 * DFF), i), sp.layer((DFF, D), i),
        ],
        out_specs=[sp.tile(sp.ffn, D), sp.group(sp.mix, HALO, D)],
        out_shape=[jax.ShapeDtypeStruct((bn, seq, D), F32), jax.ShapeDtypeStruct((bn, HALO, D), F32)],
        scratch_shapes=[
            pltpu.VMEM((ns, r + HALO, D), F32), pltpu.VMEM((tm, D), BF),
            pltpu.VMEM((tm, D), F32), pltpu.VMEM((tm, D), F32),
            pltpu.VMEM((tm, D), BF), pltpu.VMEM((tm, D), BF),
            pltpu.VMEM((tm, DFF), BF),
        ],
        compiler_params=_layer_params(),
        name="pool_layer",
    )(x, mod, mod, hist, wts["g_mix"], wts["g_ffn"], wts["w_pool"], wts["pool_scale"],
      wts["w_ffn_in"], wts["w_ffn_out"])


def _attn_layer_call(x, mod, khist, vthist, kv_in, wts, i, j, *, ns, r, pos0):
    bn, seq, _ = x.shape
    n_tiles = seq // r
    n_steps = (bn // ns) * n_tiles
    tm = ns * r
    nh = ns * WIN
    tr = min(r, WIN)
    make_kv = len(kv_in) == 1
    assert ns == 1 or (n_tiles == 1 and r <= CHUNK)
    sp = _Specs(ns, r, n_tiles, n_steps)
    ins = [
        ("x", x, sp.tile(sp.mix, D)), ("mod", mod, sp.group(sp.mix, 6, D)), ("modc", mod, sp.group(sp.ffn, 6, D)),
        ("khist", khist, sp.k_rows(sp.mix, WIN)), ("vthist", vthist, sp.vt_cols(sp.mix, WIN)),
        ("sinks", wts["sinks"], pl.BlockSpec(memory_space=pltpu.SMEM)),
        ("gmix", wts["g_mix"], sp.layer((1, D), i)), ("gffn", wts["g_ffn"], sp.layer((1, D), i)),
        ("wqt", wts["w_qt"], sp.layer((D, D), j)), ("gq", wts["g_q"], sp.layer((HD, LANES), j)),
        ("wo", wts["w_o"], sp.layer((D, D), j)),
        ("win", wts["w_ffn_in"], sp.layer((D, 2 * DFF), i)), ("wout", wts["w_ffn_out"], sp.layer((DFF, D), i)),
    ]
    outs = [("o", jax.ShapeDtypeStruct((bn, seq, D), F32), sp.tile(sp.ffn, D))]
    if make_kv:
        ins += [("modkv", kv_in[0], sp.group(sp.mix, 2, D)), ("gkv", wts["g_kv"], sp.const((1, D))),
                ("wkv", wts["w_kv"], sp.const((D, 2 * KVD))), ("gk", wts["g_k"], sp.const((1, HD)))]
        outs += [
            ("ktail", jax.ShapeDtypeStruct((bn, tr, KVD), F32), sp.group(sp.mix, tr, KVD)),
            ("vtail", jax.ShapeDtypeStruct((bn, tr, KVD), F32), sp.group(sp.mix, tr, KVD)),
            ("kbf", jax.ShapeDtypeStruct((NKV, bn * seq, HD), BF), sp.k_rows(sp.mix)),
            ("vt", jax.ShapeDtypeStruct((KVD, bn * seq), BF), sp.vt_cols(sp.mix)),
        ]
    else:
        ins += [("knew", kv_in[0], sp.k_rows(sp.mix)), ("vtnew", kv_in[1], sp.vt_cols(sp.mix))]
    scratch = [
        ("kext", pltpu.VMEM((NKV, nh + tm, HD), BF)), ("vtext", pltpu.VMEM((KVD, nh + tm), BF)),
        ("hb", pltpu.VMEM((tm, D), BF)), ("qt", pltpu.VMEM((D, tm), BF)), ("attnt", pltpu.VMEM((D, tm), BF)),
        ("pt", pltpu.VMEM((PT_SLOTS, WIN + LANES if ns == 1 else nh + tm, GQA * LANES), BF)),
        ("x1a", pltpu.VMEM((tm, D), F32)), ("x1b", pltpu.VMEM((tm, D), F32)),
        ("h2a", pltpu.VMEM((tm, D), BF)), ("h2b", pltpu.VMEM((tm, D), BF)),
        ("act", pltpu.VMEM((tm, DFF), BF)),
    ]
    if make_kv:
        scratch.append(("hkv", pltpu.VMEM((tm, D), BF)))
    names = [e[0] for e in ins] + [e[0] for e in outs] + [e[0] for e in scratch]
    return pl.pallas_call(
        functools.partial(_attn_layer_kernel, names=names, ns=ns, r=r, pos0=pos0, n_tiles=n_tiles,
                          n_steps=n_steps, layer=j, make_kv=make_kv),
        grid=(n_steps + 1,),
        in_specs=[e[2] for e in ins],
        out_specs=[e[2] for e in outs],
        out_shape=[e[1] for e in outs],
        scratch_shapes=[e[1] for e in scratch],
        compiler_params=_layer_params(),
        name="attn_layer_kv" if make_kv else "attn_layer",
    )(*[e[1] for e in ins])


def _trunk(x, mod, modkv, pool_hist, k_hist, v_hist, wts, *, ns, r, pos0):
    n_a = pool_hist.shape[0]
    bn = x.shape[0]
    tails = []
    for i in range(n_a):
        x, tail = _pool_layer_call(x, mod[i], pool_hist[i], wts, i, ns=ns, r=r, pos0=pos0)
        tails.append(tail[:, 1:, :])
    khist = k_hist.transpose(2, 0, 1, 3).reshape(NKV, bn * WIN, HD)
    vthist = v_hist.transpose(2, 3, 0, 1).reshape(KVD, bn * WIN)
    x, ktail, vtail, kbf, vt = _attn_layer_call(x, mod[n_a], khist, vthist, (modkv,), wts, n_a, 0,
                                                ns=ns, r=r, pos0=pos0)
    for j in range(1, wts["w_qt"].shape[0]):
        x, = _attn_layer_call(x, mod[n_a + j], khist, vthist, (kbf, vt), wts, n_a + j, j, ns=ns, r=r, pos0=pos0)
    k_tail = jnp.concatenate([k_hist.reshape(bn, WIN, KVD), ktail], axis=1)[:, -WIN:].reshape(bn, WIN, NKV, HD)
    v_tail = jnp.concatenate([v_hist.reshape(bn, WIN, KVD), vtail], axis=1)[:, -WIN:].reshape(bn, WIN, NKV, HD)
    return x, jnp.stack(tails, axis=0), k_tail, v_tail


def kernel(x_prompt, x_sample, c_prompt, c_sample, state_pool, cache_k, cache_v, w_ada, b_ada, g_mix, g_ffn, w_pool, pool_scale, w_q, g_q, sinks, w_o, g_kv, w_ada_kv, b_ada_kv, w_kv, g_k, w_ffn_in, w_ffn_out):
    depth = w_ada.shape[0]
    n_a = state_pool.shape[0]
    bp = x_prompt.shape[0]
    bs, ls, _ = x_sample.shape

    c_all = jnp.concatenate([c_prompt, c_sample, jnp.zeros((16 - bp - bs, D), F32)], axis=0)
    ada = _ada_call(c_all, w_ada, b_ada).reshape(depth, 16, 6, D)
    ada_kv = _ada_call(c_all, w_ada_kv[None], b_ada_kv[None]).reshape(16, 2, D)

    wts = dict(
        g_mix=g_mix.reshape(depth, 1, D), g_ffn=g_ffn.reshape(depth, 1, D),
        w_pool=w_pool.astype(BF), pool_scale=pool_scale.reshape(n_a, 1, D),
        w_qt=w_q.astype(BF).transpose(0, 2, 1),
        g_q=jnp.broadcast_to(g_q[:, :, None], g_q.shape + (LANES,)),
        sinks=sinks, w_o=w_o.astype(BF),
        g_kv=g_kv.reshape(1, D), w_kv=w_kv.astype(BF), g_k=g_k.reshape(1, HD),
        w_ffn_in=w_ffn_in.astype(BF), w_ffn_out=w_ffn_out.astype(BF),
    )

    zero_pool = jnp.zeros((n_a, bp, HALO, D), F32)
    zero_kv = jnp.zeros((bp, WIN, NKV, HD), F32)
    y_p, pool_p, k_p, v_p = _trunk(
        x_prompt, ada[:, :bp], ada_kv[:bp], zero_pool, zero_kv, zero_kv, wts,
        ns=1, r=PROMPT_TILE_ROWS, pos0=0)

    pool_hist_s = jnp.pad(state_pool, ((0, 0), (0, 0), (HALO - state_pool.shape[2], 0), (0, 0)))
    y_s, pool_s, k_s, v_s = _trunk(
        x_sample, ada[:, bp:bp + bs], ada_kv[bp:bp + bs], pool_hist_s, cache_k, cache_v, wts,
        ns=bs, r=ls, pos0=PAST_LEN)
    return (y_p, y_s, pool_p, k_p, v_p, pool_s, k_s, v_s)
```

```python
import functools

import jax
import jax.numpy as jnp
from jax import lax
from jax.experimental import pallas as pl
from jax.experimental.pallas import tpu as pltpu

D = 1024
DFF = 2816
HD = 64
NH = 16
NKV = 4
GQA = NH // NKV
KVD = NKV * HD
WIN = 128
CHUNK = 64
PAST_LEN = 4096
POOL_WINDOWS = (2, 4, 8, 16)
PGD = D // len(POOL_WINDOWS)
HALO = 16
LANES = 128
EPS = 1e-6
NEG_INF = -1e30
BF = jnp.bfloat16
F32 = jnp.float32

V7X_VMEM_LIMIT_BYTES = 60 * 1024 * 1024
PROMPT_TILE_ROWS = 512
FFN_COL_CHUNK = 256
ROW_PIECE = 128
FFN_OUT_ROWS = 512
PT_SLOTS = 3
ADA_COL_BLOCKS = (1536, 1024, 128)

NT_DIMS = (((1,), (1,)), ((), ()))
TN_DIMS = (((0,), (0,)), ((), ()))


def _rms(x, g):
    return x * lax.rsqrt(jnp.mean(x * x, axis=-1, keepdims=True) + EPS) * g


def _sigmoid(x):
    return 1.0 / (1.0 + jnp.exp(-x))


def _mod(mod_ref, k, ns, r):
    if ns == 1:
        return mod_ref[0, k:k + 1, :]
    return jnp.concatenate([jnp.broadcast_to(mod_ref[s, k:k + 1, :], (r, D)) for s in range(ns)], axis=0)


def _row_blocks(ns, r):
    tm = ns * r
    rb = min(tm, ROW_PIECE)
    assert ns == 1 or rb == tm
    return rb, [slice(b * rb, (b + 1) * rb) for b in range(tm // rb)]


def _load_rows(ref, rows, ns, r):
    return ref[0, rows, :] if ns == 1 else ref[...].reshape(ns * r, ref.shape[-1])


def _store_rows(ref, rows, val, ns, r):
    if ns == 1:
        ref[0, rows, :] = val
    else:
        ref[...] = val.reshape(ns, r, ref.shape[-1])


def _run(pieces):
    for piece in pieces:
        piece()


def _interleave(main, side):
    merged = [((i + 0.5) / len(main), 0, f) for i, f in enumerate(main)]
    merged += [((i + 0.5) / len(side), 1, f) for i, f in enumerate(side)]
    _run(f for _, _, f in sorted(merged, key=lambda e: e[:2]))


def _skewed_steps(n, first, odd, even):
    pl.when(n == 0)(first)
    pl.when(lax.rem(n, 2) == 1)(odd)
    pl.when((n > 0) & (lax.rem(n, 2) == 0))(even)


def _ffn_pieces(x1_ref, h2_ref, modc_ref, win_ref, wout_ref, act_ref, o_ref, *, ns, r):
    tm = ns * r
    rb = min(tm, FFN_OUT_ROWS)
    blocks = [slice(b * rb, (b + 1) * rb) for b in range(tm // rb)]

    def chunk(c):
        lo = c * FFN_COL_CHUNK
        hi = lo + FFN_COL_CHUNK
        gate = jnp.dot(h2_ref[...], win_ref[0, :, lo:hi], preferred_element_type=F32)
        up = jnp.dot(h2_ref[...], win_ref[0, :, DFF + lo:DFF + hi], preferred_element_type=F32)
        act_ref[:, lo:hi] = (gate * _sigmoid(gate) * up).astype(BF)

    def rows_out(rows):
        y = jnp.dot(act_ref[rows, :], wout_ref[0], preferred_element_type=F32)
        g2 = _mod(modc_ref, 5, ns, r)
        _store_rows(o_ref, rows, x1_ref[rows, :] + g2 * y, ns, r)

    return ([functools.partial(chunk, c) for c in range(DFF // FFN_COL_CHUNK)]
            + [functools.partial(rows_out, rows) for rows in blocks])


def _lagged(firsts, seconds, lag):
    out = []
    for k in range(len(firsts) + lag):
        if k < len(firsts):
            out.append(firsts[k])
        if k >= lag:
            out.append(seconds[k - lag])
    return out


def _pool_mixer_pieces(x_ref, mod_ref, hist_ref, gmix_ref, gffn_ref, wpool_ref, pscale_ref,
                       tail_ref, ext_ref, diff_ref, x1_ref, h2_ref, t, *, ns, r, pos0, n_tiles):
    rp = min(r, ROW_PIECE)

    def pool(s, p):
        lo = p * rp
        rows = slice(s * r + lo, s * r + lo + rp)

        def m(k):
            return mod_ref[s, k:k + 1, :]

        if p == 0:
            halo = hist_ref[s]
            if n_tiles > 1:
                halo = jnp.where(t == 0, halo, ext_ref[s, r:r + HALO, :])
            ext_ref[s, 0:HALO, :] = halo
        x = x_ref[s, lo:lo + rp, :]
        h = _rms(x, gmix_ref[0]) * (1.0 + m(1)) + m(0)
        ext_ref[s, HALO + lo:HALO + lo + rp, :] = h
        if lo + rp == r:
            tail_ref[s] = h[rp - HALO:, :]

        pos = pos0 + t * r + lo + lax.broadcasted_iota(jnp.int32, (rp, 1), 0)
        for g, w in enumerate(POOL_WINDOWS):
            cols = slice(g * PGD, (g + 1) * PGD)
            e = ext_ref[s, lo:lo + rp + HALO, cols]
            acc = e
            step = 1
            while step < w:
                acc = acc + pltpu.roll(acc, step, 0)
                step *= 2
            count = jnp.minimum(w, pos + 1).astype(F32)
            pooled = acc[HALO:, :] / count
            diff_ref[rows, cols] = (pooled - e[HALO:, :]).astype(BF)

    def mix(s, p):
        lo = p * rp
        rows = slice(s * r + lo, s * r + lo + rp)

        def m(k):
            return mod_ref[s, k:k + 1, :]

        mixed = jnp.concatenate(
            [jnp.dot(diff_ref[rows, g * PGD:(g + 1) * PGD], wpool_ref[0, g], preferred_element_type=F32)
             for g in range(len(POOL_WINDOWS))], axis=1)
        x1 = x_ref[s, lo:lo + rp, :] + m(2) * (mixed * pscale_ref[0])
        x1_ref[rows, :] = x1
        h2_ref[rows, :] = (_rms(x1, gffn_ref[0]) * (1.0 + m(4)) + m(3)).astype(BF)

    sp = [(s, p) for s in range(ns) for p in range(r // rp)]
    return _lagged([functools.partial(pool, *e) for e in sp], [functools.partial(mix, *e) for e in sp], 1)


def _pool_layer_kernel(x_ref, mod_ref, modc_ref, hist_ref, gmix_ref, gffn_ref, wpool_ref, pscale_ref,
                       win_ref, wout_ref, o_ref, tail_ref,
                       ext_ref, diff_ref, x1a_ref, x1b_ref, h2a_ref, h2b_ref, act_ref,
                       *, ns, r, pos0, n_tiles, n_steps):
    n = pl.program_id(0)
    t = lax.rem(jnp.minimum(n, n_steps - 1), n_tiles)

    def mixer(x1_ref, h2_ref):
        return _pool_mixer_pieces(x_ref, mod_ref, hist_ref, gmix_ref, gffn_ref, wpool_ref, pscale_ref,
                                  tail_ref, ext_ref, diff_ref, x1_ref, h2_ref, t,
                                  ns=ns, r=r, pos0=pos0, n_tiles=n_tiles)

    def step(x1_cur, h2_cur, x1_nxt, h2_nxt):
        _interleave(_ffn_pieces(x1_cur, h2_cur, modc_ref, win_ref, wout_ref, act_ref, o_ref, ns=ns, r=r),
                    mixer(x1_nxt, h2_nxt))

    _skewed_steps(n, lambda: _run(mixer(x1a_ref, h2a_ref)),
                  lambda: step(x1a_ref, h2a_ref, x1b_ref, h2b_ref),
                  lambda: step(x1b_ref, h2b_ref, x1a_ref, h2a_ref))


def _attn_masks(ns, r, pos0, t, nk, k0, q0):
    kq = k0 + lax.broadcasted_iota(jnp.int32, (nk, LANES), 0)
    qq = q0 + lax.broadcasted_iota(jnp.int32, (nk, LANES), 1)
    if ns == 1:
        d = kq // CHUNK - qq // CHUNK
        first_valid = jnp.maximum(0, WIN - (pos0 + t * r))
        return (d >= 0) & (d <= WIN // CHUNK) & (kq >= first_valid)
    nh = ns * WIN
    is_hist = kq < nh
    key_seq = jnp.where(is_hist, kq // WIN, (kq - nh) // r)
    ok = key_seq == qq // r
    if pos0 < WIN:
        ok = ok & (~is_hist | (kq % WIN >= WIN - pos0))
    return ok


def _softmax_t(k_blk, qt_blks, mask, sinks):
    s = jnp.dot(k_blk, jnp.concatenate(qt_blks, axis=1), preferred_element_type=F32)
    ps, inv = [], []
    for g in range(GQA):
        sg = jnp.where(mask, s[:, g * LANES:(g + 1) * LANES], NEG_INF)
        sink = jnp.full((1, LANES), sinks[g], F32)
        m = jnp.maximum(jnp.max(sg, axis=0, keepdims=True), sink)
        p = jnp.exp(sg - m)
        inv.append(1.0 / (jnp.sum(p, axis=0, keepdims=True) + jnp.exp(sink - m)))
        ps.append(p.astype(BF))
    return jnp.concatenate(ps, axis=1), inv


def _attn_mixer_pieces(a, x1_ref, h2_ref, t, *, ns, r, pos0, n_tiles, layer, make_kv):
    tm = ns * r
    nh = ns * WIN
    _, blocks = _row_blocks(ns, r)
    x_ref, mod_ref, kext_ref, vtext_ref = a["x"], a["mod"], a["kext"], a["vtext"]
    hb_ref, qt_ref, attnt_ref = a["hb"], a["qt"], a["attnt"]

    def setup():
        kprev = a["khist"][...].astype(BF)
        vtprev = a["vthist"][...].astype(BF)
        if n_tiles > 1:
            kprev = jnp.where(t == 0, kprev, kext_ref[:, r:r + WIN, :])
            vtprev = jnp.where(t == 0, vtprev, vtext_ref[:, r:r + WIN])
        kext_ref[:, 0:nh, :] = kprev
        vtext_ref[:, 0:nh] = vtprev
        if not make_kv:
            kext_ref[:, nh:, :] = a["knew"][...]
            vtext_ref[:, nh:] = a["vtnew"][...]

    def rows_in(rows):
        x = _load_rows(x_ref, rows, ns, r)
        xn = x * lax.rsqrt(jnp.mean(x * x, axis=-1, keepdims=True) + EPS)
        hb_ref[rows, :] = (xn * a["gmix"][0] * (1.0 + _mod(mod_ref, 1, ns, r)) + _mod(mod_ref, 0, ns, r)).astype(BF)
        if make_kv:
            modkv_ref = a["modkv"]
            a["hkv"][rows, :] = (
                xn * a["gkv"][...] * (1.0 + _mod(modkv_ref, 1, ns, r)) + _mod(modkv_ref, 0, ns, r)).astype(BF)

    def rows_kv(rows):
        kv = jnp.dot(a["hkv"][rows, :], a["wkv"][...], preferred_element_type=F32)
        v = kv[:, KVD:]
        ks = [_rms(kv[:, j * HD:(j + 1) * HD], a["gk"][...]) for j in range(NKV)]
        ext_rows = slice(nh + rows.start, nh + rows.stop)
        for j in range(NKV):
            kj = ks[j].astype(BF)
            kext_ref[j, ext_rows, :] = kj
            a["kbf"][j, rows, :] = kj
        vt = v.T.astype(BF)
        vtext_ref[:, ext_rows] = vt
        a["vt"][:, rows] = vt
        tr = a["ktail"].shape[1]
        if ns > 1:
            for j in range(NKV):
                a["ktail"][:, :, j * HD:(j + 1) * HD] = ks[j].reshape(ns, r, HD)[:, r - tr:, :]
            a["vtail"][...] = v.reshape(ns, r, KVD)[:, r - tr:, :]
        elif rows.stop == r:
            for j in range(NKV):
                a["ktail"][0, :, j * HD:(j + 1) * HD] = ks[j]
            a["vtail"][0] = v

    def q_heads(j):
        qt = lax.dot_general(a["wqt"][0, j * GQA * HD:(j + 1) * GQA * HD, :], hb_ref[...], NT_DIMS,
                             preferred_element_type=F32)
        gq = jnp.concatenate([a["gq"][0]] * (tm // LANES), axis=1)
        for g in range(GQA):
            blk = qt[g * HD:(g + 1) * HD, :]
            inv = lax.rsqrt(jnp.mean(blk * blk, axis=0, keepdims=True) + EPS)
            hd = j * GQA + g
            qt_ref[hd * HD:(hd + 1) * HD, :] = (blk * inv * gq * (HD ** -0.5)).astype(BF)

    inv_denoms = {}

    def scores(k, j, q0, k0, nk):
        mask = _attn_masks(ns, r, pos0, t, nk, k0, q0)
        heads = [j * GQA + g for g in range(GQA)]
        pt, inv_denoms[k] = _softmax_t(
            kext_ref[j, k0:k0 + nk, :], [qt_ref[hd * HD:(hd + 1) * HD, q0:q0 + LANES] for hd in heads], mask,
            [a["sinks"][layer, hd] for hd in heads])
        a["pt"][k % PT_SLOTS] = pt

    def values(k, j, q0, k0, nk):
        o = jnp.dot(vtext_ref[j * HD:(j + 1) * HD, k0:k0 + nk], a["pt"][k % PT_SLOTS],
                    preferred_element_type=F32)
        for g, inv in enumerate(inv_denoms.pop(k)):
            hd = j * GQA + g
            attnt_ref[hd * HD:(hd + 1) * HD, q0:q0 + LANES] = (o[:, g * LANES:(g + 1) * LANES] * inv).astype(BF)

    def rows_mix(rows):
        mix = lax.dot_general(attnt_ref[:, rows], a["wo"][0], TN_DIMS, preferred_element_type=F32)
        x1 = _load_rows(x_ref, rows, ns, r) + _mod(mod_ref, 2, ns, r) * mix
        x1_ref[rows, :] = x1
        h2_ref[rows, :] = (_rms(x1, a["gffn"][0]) * (1.0 + _mod(mod_ref, 4, ns, r)) + _mod(mod_ref, 3, ns, r)).astype(BF)

    if ns == 1:
        windows = [(q0, q0, WIN + LANES) for q0 in range(0, tm, LANES)]
    else:
        windows = [(0, 0, nh + tm)]
    pieces = [setup] + [functools.partial(rows_in, rows) for rows in blocks]
    if make_kv:
        pieces += [functools.partial(rows_kv, rows) for rows in blocks]
    pieces += [functools.partial(q_heads, j) for j in range(NKV)]
    att = [(k, j) + w for k, (j, w) in enumerate((j, w) for j in range(NKV) for w in windows)]
    pieces += _lagged([functools.partial(scores, *e) for e in att], [functools.partial(values, *e) for e in att],
                      PT_SLOTS - 1)
    return pieces + [functools.partial(rows_mix, rows) for rows in blocks]


def _attn_layer_kernel(*refs, names, ns, r, pos0, n_tiles, n_steps, layer, make_kv):
    a = dict(zip(names, refs))
    n = pl.program_id(0)
    t = lax.rem(jnp.minimum(n, n_steps - 1), n_tiles)

    def mixer(x1_ref, h2_ref):
        return _attn_mixer_pieces(a, x1_ref, h2_ref, t, ns=ns, r=r, pos0=pos0, n_tiles=n_tiles,
                                  layer=layer, make_kv=make_kv)

    def step(x1_cur, h2_cur, x1_nxt, h2_nxt):
        _interleave(_ffn_pieces(x1_cur, h2_cur, a["modc"], a["win"], a["wout"], a["act"], a["o"], ns=ns, r=r),
                    mixer(x1_nxt, h2_nxt))

    _skewed_steps(n, lambda: _run(mixer(a["x1a"], a["h2a"])),
                  lambda: step(a["x1a"], a["h2a"], a["x1b"], a["h2b"]),
                  lambda: step(a["x1b"], a["h2b"], a["x1a"], a["h2a"]))


def _ada_kernel(c_ref, w_ref, b_ref, o_ref):
    c = c_ref[...]
    c_act = (c * _sigmoid(c)).astype(BF)
    o_ref[0] = jnp.dot(c_act, w_ref[0].astype(BF), preferred_element_type=F32) + b_ref[0]


def _ada_call(c_all, w, b):
    nl, _, n = w.shape
    m = c_all.shape[0]
    nb = next(c for c in ADA_COL_BLOCKS if n % c == 0)
    return pl.pallas_call(
        _ada_kernel,
        grid=(nl, n // nb),
        in_specs=[
            pl.BlockSpec((m, D), lambda l, j: (0, 0)),
            pl.BlockSpec((1, D, nb), lambda l, j: (l, 0, j)),
            pl.BlockSpec((1, 1, nb), lambda l, j: (l, 0, j)),
        ],
        out_specs=pl.BlockSpec((1, m, nb), lambda l, j: (l, 0, j)),
        out_shape=jax.ShapeDtypeStruct((nl, m, n), F32),
        compiler_params=pltpu.CompilerParams(
            dimension_semantics=("arbitrary", "arbitrary"), vmem_limit_bytes=V7X_VMEM_LIMIT_BYTES),
        name="ada",
    )(c_all, w, b.reshape(nl, 1, n))


class _Specs:
    def __init__(self, ns, r, n_tiles, n_steps):
        self.ns, self.r, self.n_tiles = ns, r, n_tiles
        self.mix = lambda n: jnp.minimum(n, n_steps - 1)
        self.ffn = lambda n: jnp.maximum(n - 1, 0)

    def tile(self, which, width):
        return pl.BlockSpec((self.ns, self.r, width),
                            lambda n: (which(n) // self.n_tiles, which(n) % self.n_tiles, 0))

    def group(self, which, rows, width):
        return pl.BlockSpec((self.ns, rows, width), lambda n: (which(n) // self.n_tiles, 0, 0))

    def k_rows(self, which, rows_per_group=None):
        if rows_per_group is None:
            return pl.BlockSpec((NKV, self.ns * self.r, HD), lambda n: (0, which(n), 0))
        return pl.BlockSpec((NKV, self.ns * rows_per_group, HD), lambda n: (0, which(n) // self.n_tiles, 0))

    def vt_cols(self, which, rows_per_group=None):
        if rows_per_group is None:
            return pl.BlockSpec((KVD, self.ns * self.r), lambda n: (0, which(n)))
        return pl.BlockSpec((KVD, self.ns * rows_per_group), lambda n: (0, which(n) // self.n_tiles))

    @staticmethod
    def const(shape):
        return pl.BlockSpec(shape, lambda n: (0,) * len(shape), pipeline_mode=pl.Buffered(1))

    @staticmethod
    def layer(shape, i):
        return pl.BlockSpec((1,) + shape, lambda n: (i,) + (0,) * len(shape), pipeline_mode=pl.Buffered(1))


def _layer_params():
    return pltpu.CompilerParams(dimension_semantics=("arbitrary",), vmem_limit_bytes=V7X_VMEM_LIMIT_BYTES)


def _pool_layer_call(x, mod, hist, wts, i, *, ns, r, pos0):
    bn, seq, _ = x.shape
    n_tiles = seq // r
    n_steps = (bn // ns) * n_tiles
    tm = ns * r
    sp = _Specs(ns, r, n_tiles, n_steps)
    return pl.pallas_call(
        functools.partial(_pool_layer_kernel, ns=ns, r=r, pos0=pos0, n_tiles=n_tiles, n_steps=n_steps),
        grid=(n_steps + 1,),
        in_specs=[
            sp.tile(sp.mix, D), sp.group(sp.mix, 6, D), sp.group(sp.ffn, 6, D), sp.group(sp.mix, HALO, D),
            sp.layer((1, D), i), sp.layer((1, D), i), sp.layer((len(POOL_WINDOWS), PGD, PGD), i),
            sp.layer((1, D), i), sp.layer((D, 2 * DFF), i), sp.layer((DFF, D), i),
        ],
        out_specs=[sp.tile(sp.ffn, D), sp.group(sp.mix, HALO, D)],
        out_shape=[jax.ShapeDtypeStruct((bn, seq, D), F32), jax.ShapeDtypeStruct((bn, HALO, D), F32)],
        scratch_shapes=[
            pltpu.VMEM((ns, r + HALO, D), F32), pltpu.VMEM((tm, D), BF),
            pltpu.VMEM((tm, D), F32), pltpu.VMEM((tm, D), F32),
            pltpu.VMEM((tm, D), BF), pltpu.VMEM((tm, D), BF),
            pltpu.VMEM((tm, DFF), BF),
        ],
        compiler_params=_layer_params(),
        name="pool_layer",
    )(x, mod, mod, hist, wts["g_mix"], wts["g_ffn"], wts["w_pool"], wts["pool_scale"],
      wts["w_ffn_in"], wts["w_ffn_out"])


def _attn_layer_call(x, mod, khist, vthist, kv_in, wts, i, j, *, ns, r, pos0):
    bn, seq, _ = x.shape
    n_tiles = seq // r
    n_steps = (bn // ns) * n_tiles
    tm = ns * r
    nh = ns * WIN
    tr = min(r, WIN)
    make_kv = len(kv_in) == 1
    assert ns == 1 or (n_tiles == 1 and r <= CHUNK)
    sp = _Specs(ns, r, n_tiles, n_steps)
    ins = [
        ("x", x, sp.tile(sp.mix, D)), ("mod", mod, sp.group(sp.mix, 6, D)), ("modc", mod, sp.group(sp.ffn, 6, D)),
        ("khist", khist, sp.k_rows(sp.mix, WIN)), ("vthist", vthist, sp.vt_cols(sp.mix, WIN)),
        ("sinks", wts["sinks"], pl.BlockSpec(memory_space=pltpu.SMEM)),
        ("gmix", wts["g_mix"], sp.layer((1, D), i)), ("gffn", wts["g_ffn"], sp.layer((1, D), i)),
        ("wqt", wts["w_qt"], sp.layer((D, D), j)), ("gq", wts["g_q"], sp.layer((HD, LANES), j)),
        ("wo", wts["w_o"], sp.layer((D, D), j)),
        ("win", wts["w_ffn_in"], sp.layer((D, 2 * DFF), i)), ("wout", wts["w_ffn_out"], sp.layer((DFF, D), i)),
    ]
    outs = [("o", jax.ShapeDtypeStruct((bn, seq, D), F32), sp.tile(sp.ffn, D))]
    if make_kv:
        ins += [("modkv", kv_in[0], sp.group(sp.mix, 2, D)), ("gkv", wts["g_kv"], sp.const((1, D))),
                ("wkv", wts["w_kv"], sp.const((D, 2 * KVD))), ("gk", wts["g_k"], sp.const((1, HD)))]
        outs += [
            ("ktail", jax.ShapeDtypeStruct((bn, tr, KVD), F32), sp.group(sp.mix, tr, KVD)),
            ("vtail", jax.ShapeDtypeStruct((bn, tr, KVD), F32), sp.group(sp.mix, tr, KVD)),
            ("kbf", jax.ShapeDtypeStruct((NKV, bn * seq, HD), BF), sp.k_rows(sp.mix)),
            ("vt", jax.ShapeDtypeStruct((KVD, bn * seq), BF), sp.vt_cols(sp.mix)),
        ]
    else:
        ins += [("knew", kv_in[0], sp.k_rows(sp.mix)), ("vtnew", kv_in[1], sp.vt_cols(sp.mix))]
    scratch = [
        ("kext", pltpu.VMEM((NKV, nh + tm, HD), BF)), ("vtext", pltpu.VMEM((KVD, nh + tm), BF)),
        ("hb", pltpu.VMEM((tm, D), BF)), ("qt", pltpu.VMEM((D, tm), BF)), ("attnt", pltpu.VMEM((D, tm), BF)),
        ("pt", pltpu.VMEM((PT_SLOTS, WIN + LANES if ns == 1 else nh + tm, GQA * LANES), BF)),
        ("x1a", pltpu.VMEM((tm, D), F32)), ("x1b", pltpu.VMEM((tm, D), F32)),
        ("h2a", pltpu.VMEM((tm, D), BF)), ("h2b", pltpu.VMEM((tm, D), BF)),
        ("act", pltpu.VMEM((tm, DFF), BF)),
    ]
    if make_kv:
        scratch.append(("hkv", pltpu.VMEM((tm, D), BF)))
    names = [e[0] for e in ins] + [e[0] for e in outs] + [e[0] for e in scratch]
    return pl.pallas_call(
        functools.partial(_attn_layer_kernel, names=names, ns=ns, r=r, pos0=pos0, n_tiles=n_tiles,
                          n_steps=n_steps, layer=j, make_kv=make_kv),
        grid=(n_steps + 1,),
        in_specs=[e[2] for e in ins],
        out_specs=[e[2] for e in outs],
        out_shape=[e[1] for e in outs],
        scratch_shapes=[e[1] for e in scratch],
        compiler_params=_layer_params(),
        name="attn_layer_kv" if make_kv else "attn_layer",
    )(*[e[1] for e in ins])


def _trunk(x, mod, modkv, pool_hist, k_hist, v_hist, wts, *, ns, r, pos0):
    n_a = pool_hist.shape[0]
    bn = x.shape[0]
    tails = []
    for i in range(n_a):
        x, tail = _pool_layer_call(x, mod[i], pool_hist[i], wts, i, ns=ns, r=r, pos0=pos0)
        tails.append(tail[:, 1:, :])
    khist = k_hist.transpose(2, 0, 1, 3).reshape(NKV, bn * WIN, HD)
    vthist = v_hist.transpose(2, 3, 0, 1).reshape(KVD, bn * WIN)
    x, ktail, vtail, kbf, vt = _attn_layer_call(x, mod[n_a], khist, vthist, (modkv,), wts, n_a, 0,
                                                ns=ns, r=r, pos0=pos0)
    for j in range(1, wts["w_qt"].shape[0]):
        x, = _attn_layer_call(x, mod[n_a + j], khist, vthist, (kbf, vt), wts, n_a + j, j, ns=ns, r=r, pos0=pos0)
    k_tail = jnp.concatenate([k_hist.reshape(bn, WIN, KVD), ktail], axis=1)[:, -WIN:].reshape(bn, WIN, NKV, HD)
    v_tail = jnp.concatenate([v_hist.reshape(bn, WIN, KVD), vtail], axis=1)[:, -WIN:].reshape(bn, WIN, NKV, HD)
    return x, jnp.stack(tails, axis=0), k_tail, v_tail


def kernel(x_prompt, x_sample, c_prompt, c_sample, state_pool, cache_k, cache_v, w_ada, b_ada, g_mix, g_ffn, w_pool, pool_scale, w_q, g_q, sinks, w_o, g_kv, w_ada_kv, b_ada_kv, w_kv, g_k, w_ffn_in, w_ffn_out):
    depth = w_ada.shape[0]
    n_a = state_pool.shape[0]
    bp = x_prompt.shape[0]
    bs, ls, _ = x_sample.shape

    c_all = jnp.concatenate([c_prompt, c_sample, jnp.zeros((16 - bp - bs, D), F32)], axis=0)
    ada = _ada_call(c_all, w_ada, b_ada).reshape(depth, 16, 6, D)
    ada_kv = _ada_call(c_all, w_ada_kv[None], b_ada_kv[None]).reshape(16, 2, D)

    wts = dict(
        g_mix=g_mix.reshape(depth, 1, D), g_ffn=g_ffn.reshape(depth, 1, D),
        w_pool=w_pool.astype(BF), pool_scale=pool_scale.reshape(n_a, 1, D),
        w_qt=w_q.astype(BF).transpose(0, 2, 1),
        g_q=jnp.broadcast_to(g_q[:, :, None], g_q.shape + (LANES,)),
        sinks=sinks, w_o=w_o.astype(BF),
        g_kv=g_kv.reshape(1, D), w_kv=w_kv.astype(BF), g_k=g_k.reshape(1, HD),
        w_ffn_in=w_ffn_in.astype(BF), w_ffn_out=w_ffn_out.astype(BF),
    )

    zero_pool = jnp.zeros((n_a, bp, HALO, D), F32)
    zero_kv = jnp.zeros((bp, WIN, NKV, HD), F32)
    y_p, pool_p, k_p, v_p = _trunk(
        x_prompt, ada[:, :bp], ada_kv[:bp], zero_pool, zero_kv, zero_kv, wts,
        ns=1, r=PROMPT_TILE_ROWS, pos0=0)

    pool_hist_s = jnp.pad(state_pool, ((0, 0), (0, 0), (HALO - state_pool.shape[2], 0), (0, 0)))
    y_s, pool_s, k_s, v_s = _trunk(
        x_sample, ada[:, bp:bp + bs], ada_kv[bp:bp + bs], pool_hist_s, cache_k, cache_v, wts,
        ns=bs, r=ls, pos0=PAST_LEN)
    return (y_p, y_s, pool_p, k_p, v_p, pool_s, k_s, v_s)
```

```python
import functools

import jax
import jax.numpy as jnp
from jax import lax
from jax.experimental import pallas as pl
from jax.experimental.pallas import tpu as pltpu

D = 1024
DFF = 2816
HD = 64
NH = 16
NKV = 4
GQA = NH // NKV
KVD = NKV * HD
WIN = 128
CHUNK = 64
PAST_LEN = 4096
POOL_WINDOWS = (2, 4, 8, 16)
PGD = D // len(POOL_WINDOWS)
HALO = 16
LANES = 128
EPS = 1e-6
NEG_INF = -1e30
BF = jnp.bfloat16
F32 = jnp.float32

V7X_VMEM_LIMIT_BYTES = 60 * 1024 * 1024
PROMPT_TILE_ROWS = 256
FFN_COL_CHUNK = 256
ROW_PIECE = 128
FFN_OUT_ROWS = 512
PT_SLOTS = 3
ADA_COL_BLOCKS = (1536, 1024, 128)

NT_DIMS = (((1,), (1,)), ((), ()))
TN_DIMS = (((0,), (0,)), ((), ()))


def _rms(x, g):
    return x * lax.rsqrt(jnp.mean(x * x, axis=-1, keepdims=True) + EPS) * g


def _sigmoid(x):
    return 1.0 / (1.0 + jnp.exp(-x))


def _mod(mod_ref, k, ns, r):
    if ns == 1:
        return mod_ref[0, k:k + 1, :]
    return jnp.concatenate([jnp.broadcast_to(mod_ref[s, k:k + 1, :], (r, D)) for s in range(ns)], axis=0)


def _row_blocks(ns, r):
    tm = ns * r
    rb = min(tm, ROW_PIECE)
    assert ns == 1 or rb == tm
    return rb, [slice(b * rb, (b + 1) * rb) for b in range(tm // rb)]


def _load_rows(ref, rows, ns, r):
    return ref[0, rows, :] if ns == 1 else ref[...].reshape(ns * r, ref.shape[-1])


def _store_rows(ref, rows, val, ns, r):
    if ns == 1:
        ref[0, rows, :] = val
    else:
        ref[...] = val.reshape(ns, r, ref.shape[-1])


def _run(pieces):
    for piece in pieces:
        piece()


def _interleave(main, side):
    merged = [((i + 0.5) / len(main), 0, f) for i, f in enumerate(main)]
    merged += [((i + 0.5) / len(side), 1, f) for i, f in enumerate(side)]
    _run(f for _, _, f in sorted(merged, key=lambda e: e[:2]))


def _skewed_steps(n, first, odd, even):
    pl.when(n == 0)(first)
    pl.when(lax.rem(n, 2) == 1)(odd)
    pl.when((n > 0) & (lax.rem(n, 2) == 0))(even)


def _ffn_pieces(x1_ref, h2_ref, modc_ref, win_ref, wout_ref, act_ref, o_ref, *, ns, r):
    tm = ns * r
    rb = min(tm, FFN_OUT_ROWS)
    blocks = [slice(b * rb, (b + 1) * rb) for b in range(tm // rb)]

    def chunk(c):
        lo = c * FFN_COL_CHUNK
        hi = lo + FFN_COL_CHUNK
        gate = jnp.dot(h2_ref[...], win_ref[0, :, lo:hi], preferred_element_type=F32)
        up = jnp.dot(h2_ref[...], win_ref[0, :, DFF + lo:DFF + hi], preferred_element_type=F32)
        act_ref[:, lo:hi] = (gate * _sigmoid(gate) * up).astype(BF)

    def rows_out(rows):
        y = jnp.dot(act_ref[rows, :], wout_ref[0], preferred_element_type=F32)
        g2 = _mod(modc_ref, 5, ns, r)
        _store_rows(o_ref, rows, x1_ref[rows, :] + g2 * y, ns, r)

    return ([functools.partial(chunk, c) for c in range(DFF // FFN_COL_CHUNK)]
            + [functools.partial(rows_out, rows) for rows in blocks])


def _lagged(firsts, seconds, lag):
    out = []
    for k in range(len(firsts) + lag):
        if k < len(firsts):
            out.append(firsts[k])
        if k >= lag:
            out.append(seconds[k - lag])
    return out


def _pool_mixer_pieces(x_ref, mod_ref, hist_ref, gmix_ref, gffn_ref, wpool_ref, pscale_ref,
                       tail_ref, ext_ref, diff_ref, x1_ref, h2_ref, t, *, ns, r, pos0, n_tiles):
    rp = min(r, ROW_PIECE)

    def pool(s, p):
        lo = p * rp
        rows = slice(s * r + lo, s * r + lo + rp)

        def m(k):
            return mod_ref[s, k:k + 1, :]

        if p == 0:
            halo = hist_ref[s]
            if n_tiles > 1:
                halo = jnp.where(t == 0, halo, ext_ref[s, r:r + HALO, :])
            ext_ref[s, 0:HALO, :] = halo
        x = x_ref[s, lo:lo + rp, :]
        h = _rms(x, gmix_ref[0]) * (1.0 + m(1)) + m(0)
        ext_ref[s, HALO + lo:HALO + lo + rp, :] = h
        if lo + rp == r:
            tail_ref[s] = h[rp - HALO:, :]

        pos = pos0 + t * r + lo + lax.broadcasted_iota(jnp.int32, (rp, 1), 0)
        for g, w in enumerate(POOL_WINDOWS):
            cols = slice(g * PGD, (g + 1) * PGD)
            e = ext_ref[s, lo:lo + rp + HALO, cols]
            acc = e
            step = 1
            while step < w:
                acc = acc + pltpu.roll(acc, step, 0)
                step *= 2
            count = jnp.minimum(w, pos + 1).astype(F32)
            pooled = acc[HALO:, :] / count
            diff_ref[rows, cols] = (pooled - e[HALO:, :]).astype(BF)

    def mix(s, p):
        lo = p * rp
        rows = slice(s * r + lo, s * r + lo + rp)

        def m(k):
            return mod_ref[s, k:k + 1, :]

        mixed = jnp.concatenate(
            [jnp.dot(diff_ref[rows, g * PGD:(g + 1) * PGD], wpool_ref[0, g], preferred_element_type=F32)
             for g in range(len(POOL_WINDOWS))], axis=1)
        x1 = x_ref[s, lo:lo + rp, :] + m(2) * (mixed * pscale_ref[0])
        x1_ref[rows, :] = x1
        h2_ref[rows, :] = (_rms(x1, gffn_ref[0]) * (1.0 + m(4)) + m(3)).astype(BF)

    sp = [(s, p) for s in range(ns) for p in range(r // rp)]
    return _lagged([functools.partial(pool, *e) for e in sp], [functools.partial(mix, *e) for e in sp], 1)


def _pool_layer_kernel(x_ref, mod_ref, modc_ref, hist_ref, gmix_ref, gffn_ref, wpool_ref, pscale_ref,
                       win_ref, wout_ref, o_ref, tail_ref,
                       ext_ref, diff_ref, x1a_ref, x1b_ref, h2a_ref, h2b_ref, act_ref,
                       *, ns, r, pos0, n_tiles, n_steps):
    n = pl.program_id(0)
    t = lax.rem(jnp.minimum(n, n_steps - 1), n_tiles)

    def mixer(x1_ref, h2_ref):
        return _pool_mixer_pieces(x_ref, mod_ref, hist_ref, gmix_ref, gffn_ref, wpool_ref, pscale_ref,
                                  tail_ref, ext_ref, diff_ref, x1_ref, h2_ref, t,
                                  ns=ns, r=r, pos0=pos0, n_tiles=n_tiles)

    def step(x1_cur, h2_cur, x1_nxt, h2_nxt):
        ffn = _ffn_pieces(x1_cur, h2_cur, modc_ref, win_ref, wout_ref, act_ref, o_ref, ns=ns, r=r)
        if n_steps == 1:
            _run(ffn)
        else:
            _interleave(ffn, mixer(x1_nxt, h2_nxt))

    _skewed_steps(n, lambda: _run(mixer(x1a_ref, h2a_ref)),
                  lambda: step(x1a_ref, h2a_ref, x1b_ref, h2b_ref),
                  lambda: step(x1b_ref, h2b_ref, x1a_ref, h2a_ref))


def _attn_masks(ns, r, pos0, t, nk, k0, q0):
    kq = k0 + lax.broadcasted_iota(jnp.int32, (nk, LANES), 0)
    qq = q0 + lax.broadcasted_iota(jnp.int32, (nk, LANES), 1)
    if ns == 1:
        d = kq // CHUNK - qq // CHUNK
        first_valid = jnp.maximum(0, WIN - (pos0 + t * r))
        return (d >= 0) & (d <= WIN // CHUNK) & (kq >= first_valid)
    nh = ns * WIN
    is_hist = kq < nh
    key_seq = jnp.where(is_hist, kq // WIN, (kq - nh) // r)
    ok = key_seq == qq // r
    if pos0 < WIN:
        ok = ok & (~is_hist | (kq % WIN >= WIN - pos0))
    return ok


def _softmax_t(k_blk, qt_blks, mask, sinks):
    s = jnp.dot(k_blk, jnp.concatenate(qt_blks, axis=1), preferred_element_type=F32)
    ps, inv = [], []
    for g in range(GQA):
        sg = jnp.where(mask, s[:, g * LANES:(g + 1) * LANES], NEG_INF)
        sink = jnp.full((1, LANES), sinks[g], F32)
        m = jnp.maximum(jnp.max(sg, axis=0, keepdims=True), sink)
        p = jnp.exp(sg - m)
        inv.append(1.0 / (jnp.sum(p, axis=0, keepdims=True) + jnp.exp(sink - m)))
        ps.append(p.astype(BF))
    return jnp.concatenate(ps, axis=1), inv


def _attn_mixer_pieces(a, x1_ref, h2_ref, t, *, ns, r, pos0, n_tiles, layer, make_kv):
    tm = ns * r
    nh = ns * WIN
    _, blocks = _row_blocks(ns, r)
    x_ref, mod_ref, kext_ref, vtext_ref = a["x"], a["mod"], a["kext"], a["vtext"]
    hb_ref, qt_ref, attnt_ref = a["hb"], a["qt"], a["attnt"]

    def setup():
        kprev = a["khist"][...].astype(BF)
        vtprev = a["vthist"][...].astype(BF)
        if n_tiles > 1:
            kprev = jnp.where(t == 0, kprev, kext_ref[:, r:r + WIN, :])
            vtprev = jnp.where(t == 0, vtprev, vtext_ref[:, r:r + WIN])
        kext_ref[:, 0:nh, :] = kprev
        vtext_ref[:, 0:nh] = vtprev
        if not make_kv:
            kext_ref[:, nh:, :] = a["knew"][...]
            vtext_ref[:, nh:] = a["vtnew"][...]

    def rows_in(rows):
        x = _load_rows(x_ref, rows, ns, r)
        xn = x * lax.rsqrt(jnp.mean(x * x, axis=-1, keepdims=True) + EPS)
        hb_ref[rows, :] = (xn * a["gmix"][0] * (1.0 + _mod(mod_ref, 1, ns, r)) + _mod(mod_ref, 0, ns, r)).astype(BF)
        if make_kv:
            modkv_ref = a["modkv"]
            a["hkv"][rows, :] = (
                xn * a["gkv"][...] * (1.0 + _mod(modkv_ref, 1, ns, r)) + _mod(modkv_ref, 0, ns, r)).astype(BF)

    def rows_kv(rows):
        kv = jnp.dot(a["hkv"][rows, :], a["wkv"][...], preferred_element_type=F32)
        v = kv[:, KVD:]
        ks = [_rms(kv[:, j * HD:(j + 1) * HD], a["gk"][...]) for j in range(NKV)]
        ext_rows = slice(nh + rows.start, nh + rows.stop)
        for j in range(NKV):
            kj = ks[j].astype(BF)
            kext_ref[j, ext_rows, :] = kj
            a["kbf"][j, rows, :] = kj
        vt = v.T.astype(BF)
        vtext_ref[:, ext_rows] = vt
        a["vt"][:, rows] = vt
        tr = a["ktail"].shape[1]
        if ns > 1:
            for j in range(NKV):
                a["ktail"][:, :, j * HD:(j + 1) * HD] = ks[j].reshape(ns, r, HD)[:, r - tr:, :]
            a["vtail"][...] = v.reshape(ns, r, KVD)[:, r - tr:, :]
        elif rows.stop == r:
            for j in range(NKV):
                a["ktail"][0, :, j * HD:(j + 1) * HD] = ks[j]
            a["vtail"][0] = v

    def q_heads(j):
        qt = lax.dot_general(a["wqt"][0, j * GQA * HD:(j + 1) * GQA * HD, :], hb_ref[...], NT_DIMS,
                             preferred_element_type=F32)
        gq = jnp.concatenate([a["gq"][0]] * (tm // LANES), axis=1)
        for g in range(GQA):
            blk = qt[g * HD:(g + 1) * HD, :]
            inv = lax.rsqrt(jnp.mean(blk * blk, axis=0, keepdims=True) + EPS)
            hd = j * GQA + g
            qt_ref[hd * HD:(hd + 1) * HD, :] = (blk * inv * gq * (HD ** -0.5)).astype(BF)

    inv_denoms = {}

    def scores(k, j, q0, k0, nk):
        mask = _attn_masks(ns, r, pos0, t, nk, k0, q0)
        heads = [j * GQA + g for g in range(GQA)]
        pt, inv_denoms[k] = _softmax_t(
            kext_ref[j, k0:k0 + nk, :], [qt_ref[hd * HD:(hd + 1) * HD, q0:q0 + LANES] for hd in heads], mask,
            [a["sinks"][layer, hd] for hd in heads])
        a["pt"][k % PT_SLOTS] = pt

    def values(k, j, q0, k0, nk):
        o = jnp.dot(vtext_ref[j * HD:(j + 1) * HD, k0:k0 + nk], a["pt"][k % PT_SLOTS],
                    preferred_element_type=F32)
        for g, inv in enumerate(inv_denoms.pop(k)):
            hd = j * GQA + g
            attnt_ref[hd * HD:(hd + 1) * HD, q0:q0 + LANES] = (o[:, g * LANES:(g + 1) * LANES] * inv).astype(BF)

    def rows_mix(rows):
        mix = lax.dot_general(attnt_ref[:, rows], a["wo"][0], TN_DIMS, preferred_element_type=F32)
        x1 = _load_rows(x_ref, rows, ns, r) + _mod(mod_ref, 2, ns, r) * mix
        x1_ref[rows, :] = x1
        h2_ref[rows, :] = (_rms(x1, a["gffn"][0]) * (1.0 + _mod(mod_ref, 4, ns, r)) + _mod(mod_ref, 3, ns, r)).astype(BF)

    if ns == 1:
        windows = [(q0, q0, WIN + LANES) for q0 in range(0, tm, LANES)]
    else:
        windows = [(0, 0, nh + tm)]
    pieces = [setup] + [functools.partial(rows_in, rows) for rows in blocks]
    if make_kv:
        pieces += [functools.partial(rows_kv, rows) for rows in blocks]
    pieces += [functools.partial(q_heads, j) for j in range(NKV)]
    att = [(k, j) + w for k, (j, w) in enumerate((j, w) for j in range(NKV) for w in windows)]
    pieces += _lagged([functools.partial(scores, *e) for e in att], [functools.partial(values, *e) for e in att],
                      PT_SLOTS - 1)
    return pieces + [functools.partial(rows_mix, rows) for rows in blocks]


def _attn_layer_kernel(*refs, names, ns, r, pos0, n_tiles, n_steps, layer, make_kv):
    a = dict(zip(names, refs))
    n = pl.program_id(0)
    t = lax.rem(jnp.minimum(n, n_steps - 1), n_tiles)

    def mixer(x1_ref, h2_ref):
        return _attn_mixer_pieces(a, x1_ref, h2_ref, t, ns=ns, r=r, pos0=pos0, n_tiles=n_tiles,
                                  layer=layer, make_kv=make_kv)

    def step(x1_cur, h2_cur, x1_nxt, h2_nxt):
        ffn = _ffn_pieces(x1_cur, h2_cur, a["modc"], a["win"], a["wout"], a["act"], a["o"], ns=ns, r=r)
        if n_steps == 1:
            _run(ffn)
        else:
            _interleave(ffn, mixer(x1_nxt, h2_nxt))

    _skewed_steps(n, lambda: _run(mixer(a["x1a"], a["h2a"])),
                  lambda: step(a["x1a"], a["h2a"], a["x1b"], a["h2b"]),
                  lambda: step(a["x1b"], a["h2b"], a["x1a"], a["h2a"]))


def _ada_kernel(c_ref, w_ref, b_ref, o_ref):
    c = c_ref[...]
    c_act = (c * _sigmoid(c)).astype(BF)
    o_ref[0] = jnp.dot(c_act, w_ref[0].astype(BF), preferred_element_type=F32) + b_ref[0]


def _ada_call(c_all, w, b):
    nl, _, n = w.shape
    m = c_all.shape[0]
    nb = next(c for c in ADA_COL_BLOCKS if n % c == 0)
    return pl.pallas_call(
        _ada_kernel,
        grid=(nl, n // nb),
        in_specs=[
            pl.BlockSpec((m, D), lambda l, j: (0, 0)),
            pl.BlockSpec((1, D, nb), lambda l, j: (l, 0, j)),
            pl.BlockSpec((1, 1, nb), lambda l, j: (l, 0, j)),
        ],
        out_specs=pl.BlockSpec((1, m, nb), lambda l, j: (l, 0, j)),
        out_shape=jax.ShapeDtypeStruct((nl, m, n), F32),
        compiler_params=pltpu.CompilerParams(
            dimension_semantics=("arbitrary", "arbitrary"), vmem_limit_bytes=V7X_VMEM_LIMIT_BYTES),
        name="ada",
    )(c_all, w, b.reshape(nl, 1, n))


class _Specs:
    def __init__(self, ns, r, n_tiles, n_steps):
        self.ns, self.r, self.n_tiles = ns, r, n_tiles
        self.mix = lambda n: jnp.minimum(n, n_steps - 1)
        self.ffn = lambda n: jnp.maximum(n - 1, 0)

    def tile(self, which, width):
        return pl.BlockSpec((self.ns, self.r, width),
                            lambda n: (which(n) // self.n_tiles, which(n) % self.n_tiles, 0))

    def group(self, which, rows, width):
        return pl.BlockSpec((self.ns, rows, width), lambda n: (which(n) // self.n_tiles, 0, 0))

    def k_rows(self, which, rows_per_group=None):
        if rows_per_group is None:
            return pl.BlockSpec((NKV, self.ns * self.r, HD), lambda n: (0, which(n), 0))
        return pl.BlockSpec((NKV, self.ns * rows_per_group, HD), lambda n: (0, which(n) // self.n_tiles, 0))

    def vt_cols(self, which, rows_per_group=None):
        if rows_per_group is None:
            return pl.BlockSpec((KVD, self.ns * self.r), lambda n: (0, which(n)))
        return pl.BlockSpec((KVD, self.ns * rows_per_group), lambda n: (0, which(n) // self.n_tiles))

    @staticmethod
    def const(shape):
        return pl.BlockSpec(shape, lambda n: (0,) * len(shape), pipeline_mode=pl.Buffered(1))

    @staticmethod
    def layer(shape, i):
        return pl.BlockSpec((1,) + shape, lambda n: (i,) + (0,) * len(shape), pipeline_mode=pl.Buffered(1))


def _layer_params():
    return pltpu.CompilerParams(dimension_semantics=("arbitrary",), vmem_limit_bytes=V7X_VMEM_LIMIT_BYTES)


def _pool_layer_call(x, mod, hist, wts, i, *, ns, r, pos0):
    bn, seq, _ = x.shape
    n_tiles = seq // r
    n_steps = (bn // ns) * n_tiles
    tm = ns * r
    sp = _Specs(ns, r, n_tiles, n_steps)
    return pl.pallas_call(
        functools.partial(_pool_layer_kernel, ns=ns, r=r, pos0=pos0, n_tiles=n_tiles, n_steps=n_steps),
        grid=(n_steps + 1,),
        in_specs=[
            sp.tile(sp.mix, D), sp.group(sp.mix, 6, D), sp.group(sp.ffn, 6, D), sp.group(sp.mix, HALO, D),
            sp.layer((1, D), i), sp.layer((1, D), i), sp.layer((len(POOL_WINDOWS), PGD, PGD), i),
            sp.layer((1, D), i), sp.layer((D, 2 * DFF), i), sp.layer((DFF, D), i),
        ],
        out_specs=[sp.tile(sp.ffn, D), sp.group(sp.mix, HALO, D)],
        out_shape=[jax.ShapeDtypeStruct((bn, seq, D), F32), jax.ShapeDtypeStruct((bn, HALO, D), F32)],
        scratch_shapes=[
            pltpu.VMEM((ns, r + HALO, D), F32), pltpu.VMEM((tm, D), BF),
            pltpu.VMEM((tm, D), F32), pltpu.VMEM((tm, D), F32),
            pltpu.VMEM((tm, D), BF), pltpu.VMEM((tm, D), BF),
            pltpu.VMEM((tm, DFF), BF),
        ],
        compiler_params=_layer_params(),
        name="pool_layer",
    )(x, mod, mod, hist, wts["g_mix"], wts["g_ffn"], wts["w_pool"], wts["pool_scale"],
      wts["w_ffn_in"], wts["w_ffn_out"])


def _attn_layer_call(x, mod, khist, vthist, kv_in, wts, i, j, *, ns, r, pos0):
    bn, seq, _ = x.shape
    n_tiles = seq // r
    n_steps = (bn // ns) * n_tiles
    tm = ns * r
    nh = ns * WIN
    tr = min(r, WIN)
    make_kv = len(kv_in) == 1
    assert ns == 1 or (n_tiles == 1 and r <= CHUNK)
    sp = _Specs(ns, r, n_tiles, n_steps)
    ins = [
        ("x", x, sp.tile(sp.mix, D)), ("mod", mod, sp.group(sp.mix, 6, D)), ("modc", mod, sp.group(sp.ffn, 6, D)),
        ("khist", khist, sp.k_rows(sp.mix, WIN)), ("vthist", vthist, sp.vt_cols(sp.mix, WIN)),
        ("sinks", wts["sinks"], pl.BlockSpec(memory_space=pltpu.SMEM)),
        ("gmix", wts["g_mix"], sp.layer((1, D), i)), ("gffn", wts["g_ffn"], sp.layer((1, D), i)),
        ("wqt", wts["w_qt"], sp.layer((D, D), j)), ("gq", wts["g_q"], sp.layer((HD, LANES), j)),
        ("wo", wts["w_o"], sp.layer((D, D), j)),
        ("win", wts["w_ffn_in"], sp.layer((D, 2 * DFF), i)), ("wout", wts["w_ffn_out"], sp.layer((DFF, D), i)),
    ]
    outs = [("o", jax.ShapeDtypeStruct((bn, seq, D), F32), sp.tile(sp.ffn, D))]
    if make_kv:
        ins += [("modkv", kv_in[0], sp.group(sp.mix, 2, D)), ("gkv", wts["g_kv"], sp.const((1, D))),
                ("wkv", wts["w_kv"], sp.const((D, 2 * KVD))), ("gk", wts["g_k"], sp.const((1, HD)))]
        outs += [
            ("ktail", jax.ShapeDtypeStruct((bn, tr, KVD), F32), sp.group(sp.mix, tr, KVD)),
            ("vtail", jax.ShapeDtypeStruct((bn, tr, KVD), F32), sp.group(sp.mix, tr, KVD)),
            ("kbf", jax.ShapeDtypeStruct((NKV, bn * seq, HD), BF), sp.k_rows(sp.mix)),
            ("vt", jax.ShapeDtypeStruct((KVD, bn * seq), BF), sp.vt_cols(sp.mix)),
        ]
    else:
        ins += [("knew", kv_in[0], sp.k_rows(sp.mix)), ("vtnew", kv_in[1], sp.vt_cols(sp.mix))]
    scratch = [
        ("kext", pltpu.VMEM((NKV, nh + tm, HD), BF)), ("vtext", pltpu.VMEM((KVD, nh + tm), BF)),
        ("hb", pltpu.VMEM((tm, D), BF)), ("qt", pltpu.VMEM((D, tm), BF)), ("attnt", pltpu.VMEM((D, tm), BF)),
        ("pt", pltpu.VMEM((PT_SLOTS, WIN + LANES if ns == 1 else nh + tm, GQA * LANES), BF)),
        ("x1a", pltpu.VMEM((tm, D), F32)), ("x1b", pltpu.VMEM((tm, D), F32)),
        ("h2a", pltpu.VMEM((tm, D), BF)), ("h2b", pltpu.VMEM((tm, D), BF)),
        ("act", pltpu.VMEM((tm, DFF), BF)),
    ]
    if make_kv:
        scratch.append(("hkv", pltpu.VMEM((tm, D), BF)))
    names = [e[0] for e in ins] + [e[0] for e in outs] + [e[0] for e in scratch]
    return pl.pallas_call(
        functools.partial(_attn_layer_kernel, names=names, ns=ns, r=r, pos0=pos0, n_tiles=n_tiles,
                          n_steps=n_steps, layer=j, make_kv=make_kv),
        grid=(n_steps + 1,),
        in_specs=[e[2] for e in ins],
        out_specs=[e[2] for e in outs],
        out_shape=[e[1] for e in outs],
        scratch_shapes=[e[1] for e in scratch],
        compiler_params=_layer_params(),
        name="attn_layer_kv" if make_kv else "attn_layer",
    )(*[e[1] for e in ins])


def _trunk(x, mod, modkv, pool_hist, k_hist, v_hist, wts, *, ns, r, pos0):
    n_a = pool_hist.shape[0]
    bn = x.shape[0]
    tails = []
    for i in range(n_a):
        x, tail = _pool_layer_call(x, mod[i], pool_hist[i], wts, i, ns=ns, r=r, pos0=pos0)
        tails.append(tail[:, 1:, :])
    khist = k_hist.transpose(2, 0, 1, 3).reshape(NKV, bn * WIN, HD)
    vthist = v_hist.transpose(2, 3, 0, 1).reshape(KVD, bn * WIN)
    x, ktail, vtail, kbf, vt = _attn_layer_call(x, mod[n_a], khist, vthist, (modkv,), wts, n_a, 0,
                                                ns=ns, r=r, pos0=pos0)
    for j in range(1, wts["w_qt"].shape[0]):
        x, = _attn_layer_call(x, mod[n_a + j], khist, vthist, (kbf, vt), wts, n_a + j, j, ns=ns, r=r, pos0=pos0)
    k_tail = jnp.concatenate([k_hist.reshape(bn, WIN, KVD), ktail], axis=1)[:, -WIN:].reshape(bn, WIN, NKV, HD)
    v_tail = jnp.concatenate([v_hist.reshape(bn, WIN, KVD), vtail], axis=1)[:, -WIN:].reshape(bn, WIN, NKV, HD)
    return x, jnp.stack(tails, axis=0), k_tail, v_tail


def kernel(x_prompt, x_sample, c_prompt, c_sample, state_pool, cache_k, cache_v, w_ada, b_ada, g_mix, g_ffn, w_pool, pool_scale, w_q, g_q, sinks, w_o, g_kv, w_ada_kv, b_ada_kv, w_kv, g_k, w_ffn_in, w_ffn_out):
    depth = w_ada.shape[0]
    n_a = state_pool.shape[0]
    bp = x_prompt.shape[0]
    bs, ls, _ = x_sample.shape

    c_all = jnp.concatenate([c_prompt, c_sample, jnp.zeros((16 - bp - bs, D), F32)], axis=0)
    ada = _ada_call(c_all, w_ada, b_ada).reshape(depth, 16, 6, D)
    ada_kv = _ada_call(c_all, w_ada_kv[None], b_ada_kv[None]).reshape(16, 2, D)

    wts = dict(
        g_mix=g_mix.reshape(depth, 1, D), g_ffn=g_ffn.reshape(depth, 1, D),
        w_pool=w_pool.astype(BF), pool_scale=pool_scale.reshape(n_a, 1, D),
        w_qt=w_q.astype(BF).transpose(0, 2, 1),
        g_q=jnp.broadcast_to(g_q[:, :, None], g_q.shape + (LANES,)),
        sinks=sinks, w_o=w_o.astype(BF),
        g_kv=g_kv.reshape(1, D), w_kv=w_kv.astype(BF), g_k=g_k.reshape(1, HD),
        w_ffn_in=w_ffn_in.astype(BF), w_ffn_out=w_ffn_out.astype(BF),
    )

    zero_pool = jnp.zeros((n_a, bp, HALO, D), F32)
    zero_kv = jnp.zeros((bp, WIN, NKV, HD), F32)
    y_p, pool_p, k_p, v_p = _trunk(
        x_prompt, ada[:, :bp], ada_kv[:bp], zero_pool, zero_kv, zero_kv, wts,
        ns=1, r=PROMPT_TILE_ROWS, pos0=0)

    pool_hist_s = jnp.pad(state_pool, ((0, 0), (0, 0), (HALO - state_pool.shape[2], 0), (0, 0)))
    y_s, pool_s, k_s, v_s = _trunk(
        x_sample, ada[:, bp:bp + bs], ada_kv[bp:bp + bs], pool_hist_s, cache_k, cache_v, wts,
        ns=bs, r=ls, pos0=PAST_LEN)
    return (y_p, y_s, pool_p, k_p, v_p, pool_s, k_s, v_s)
```

```python
import functools

import jax
import jax.numpy as jnp
from jax import lax
from jax.experimental import pallas as pl
from jax.experimental.pallas import tpu as pltpu

D = 1024
DFF = 2816
HD = 64
NH = 16
NKV = 4
GQA = NH // NKV
KVD = NKV * HD
WIN = 128
CHUNK = 64
PAST_LEN = 4096
POOL_WINDOWS = (2, 4, 8, 16)
PGD = D // len(POOL_WINDOWS)
V_AUG_ROWS = 16
HALO = 16
LANES = 128
EPS = 1e-6
NEG_INF = -1e30
BF = jnp.bfloat16
F32 = jnp.float32

V7X_VMEM_LIMIT_BYTES = 60 * 1024 * 1024
PROMPT_TILE_ROWS = 256
FFN_COL_CHUNK = 256
ROW_PIECE = 128
FFN_OUT_ROWS = 512
PT_SLOTS = 3
ADA_COL_BLOCKS = (1536, 1024, 128)

NT_DIMS = (((1,), (1,)), ((), ()))
TN_DIMS = (((0,), (0,)), ((), ()))


def _rms(x, g):
    return x * lax.rsqrt(jnp.mean(x * x, axis=-1, keepdims=True) + EPS) * g


def _sigmoid(x):
    return 1.0 / (1.0 + jnp.exp(-x))


def _mod(mod_ref, k, ns, r):
    if ns == 1:
        return mod_ref[0, k:k + 1, :]
    return jnp.concatenate([jnp.broadcast_to(mod_ref[s, k:k + 1, :], (r, D)) for s in range(ns)], axis=0)


def _row_blocks(ns, r):
    tm = ns * r
    rb = min(tm, ROW_PIECE)
    assert ns == 1 or rb == tm
    return rb, [slice(b * rb, (b + 1) * rb) for b in range(tm // rb)]


def _load_rows(ref, rows, ns, r):
    return ref[0, rows, :] if ns == 1 else ref[...].reshape(ns * r, ref.shape[-1])


def _store_rows(ref, rows, val, ns, r):
    if ns == 1:
        ref[0, rows, :] = val
    else:
        ref[...] = val.reshape(ns, r, ref.shape[-1])


def _run(pieces):
    for piece in pieces:
        piece()


def _interleave(main, side):
    merged = [((i + 0.5) / len(main), 0, f) for i, f in enumerate(main)]
    merged += [((i + 0.5) / len(side), 1, f) for i, f in enumerate(side)]
    _run(f for _, _, f in sorted(merged, key=lambda e: e[:2]))


def _skewed_steps(n, first, odd, even):
    pl.when(n == 0)(first)
    pl.when(lax.rem(n, 2) == 1)(odd)
    pl.when((n > 0) & (lax.rem(n, 2) == 0))(even)


def _ffn_pieces(x1_ref, h2_ref, modc_ref, win_ref, wout_ref, act_ref, o_ref, *, ns, r):
    tm = ns * r
    rb = min(tm, FFN_OUT_ROWS)
    blocks = [slice(b * rb, (b + 1) * rb) for b in range(tm // rb)]

    def chunk(c):
        lo = c * FFN_COL_CHUNK
        hi = lo + FFN_COL_CHUNK
        gate = jnp.dot(h2_ref[...], win_ref[0, :, lo:hi], preferred_element_type=F32)
        up = jnp.dot(h2_ref[...], win_ref[0, :, DFF + lo:DFF + hi], preferred_element_type=F32)
        act_ref[:, lo:hi] = (gate * _sigmoid(gate) * up).astype(BF)

    def rows_out(rows):
        y = jnp.dot(act_ref[rows, :], wout_ref[0], preferred_element_type=F32)
        g2 = _mod(modc_ref, 5, ns, r)
        _store_rows(o_ref, rows, x1_ref[rows, :] + g2 * y, ns, r)

    return ([functools.partial(chunk, c) for c in range(DFF // FFN_COL_CHUNK)]
            + [functools.partial(rows_out, rows) for rows in blocks])


def _lagged(firsts, seconds, lag):
    out = []
    for k in range(len(firsts) + lag):
        if k < len(firsts):
            out.append(firsts[k])
        if k >= lag:
            out.append(seconds[k - lag])
    return out


def _pool_mixer_pieces(x_ref, mod_ref, hist_ref, gmix_ref, gffn_ref, wpool_ref, pscale_ref,
                       tail_ref, ext_ref, diff_ref, x1_ref, h2_ref, t, *, ns, r, pos0, n_tiles):
    rp = min(r, ROW_PIECE)

    def pool(s, p):
        lo = p * rp
        rows = slice(s * r + lo, s * r + lo + rp)

        def m(k):
            return mod_ref[s, k:k + 1, :]

        if p == 0:
            halo = hist_ref[s]
            if n_tiles > 1:
                halo = jnp.where(t == 0, halo, ext_ref[s, r:r + HALO, :])
            ext_ref[s, 0:HALO, :] = halo
        x = x_ref[s, lo:lo + rp, :]
        h = _rms(x, gmix_ref[0]) * (1.0 + m(1)) + m(0)
        ext_ref[s, HALO + lo:HALO + lo + rp, :] = h
        if lo + rp == r:
            tail_ref[s] = h[rp - HALO:, :]

        pos = pos0 + t * r + lo + lax.broadcasted_iota(jnp.int32, (rp, 1), 0)
        for g, w in enumerate(POOL_WINDOWS):
            cols = slice(g * PGD, (g + 1) * PGD)
            e = ext_ref[s, lo:lo + rp + HALO, cols]
            acc = e
            step = 1
            while step < w:
                acc = acc + pltpu.roll(acc, step, 0)
                step *= 2
            count = jnp.minimum(w, pos + 1).astype(F32)
            pooled = acc[HALO:, :] / count
            diff_ref[rows, cols] = (pooled - e[HALO:, :]).astype(BF)

    def mix(s, p):
        lo = p * rp
        rows = slice(s * r + lo, s * r + lo + rp)

        def m(k):
            return mod_ref[s, k:k + 1, :]

        mixed = jnp.concatenate(
            [jnp.dot(diff_ref[rows, g * PGD:(g + 1) * PGD], wpool_ref[0, g], preferred_element_type=F32)
             for g in range(len(POOL_WINDOWS))], axis=1)
        x1 = x_ref[s, lo:lo + rp, :] + m(2) * (mixed * pscale_ref[0])
        x1_ref[rows, :] = x1
        h2_ref[rows, :] = (_rms(x1, gffn_ref[0]) * (1.0 + m(4)) + m(3)).astype(BF)

    sp = [(s, p) for s in range(ns) for p in range(r // rp)]
    return _lagged([functools.partial(pool, *e) for e in sp], [functools.partial(mix, *e) for e in sp], 1)


def _pool_layer_kernel(x_ref, mod_ref, modc_ref, hist_ref, gmix_ref, gffn_ref, wpool_ref, pscale_ref,
                       win_ref, wout_ref, o_ref, tail_ref,
                       ext_ref, diff_ref, x1a_ref, x1b_ref, h2a_ref, h2b_ref, act_ref,
                       *, ns, r, pos0, n_tiles, n_steps):
    n = pl.program_id(0)
    t = lax.rem(jnp.minimum(n, n_steps - 1), n_tiles)

    def mixer(x1_ref, h2_ref):
        return _pool_mixer_pieces(x_ref, mod_ref, hist_ref, gmix_ref, gffn_ref, wpool_ref, pscale_ref,
                                  tail_ref, ext_ref, diff_ref, x1_ref, h2_ref, t,
                                  ns=ns, r=r, pos0=pos0, n_tiles=n_tiles)

    def step(x1_cur, h2_cur, x1_nxt, h2_nxt):
        ffn = _ffn_pieces(x1_cur, h2_cur, modc_ref, win_ref, wout_ref, act_ref, o_ref, ns=ns, r=r)
        if n_steps == 1:
            _run(ffn)
        else:
            _interleave(ffn, mixer(x1_nxt, h2_nxt))

    _skewed_steps(n, lambda: _run(mixer(x1a_ref, h2a_ref)),
                  lambda: step(x1a_ref, h2a_ref, x1b_ref, h2b_ref),
                  lambda: step(x1b_ref, h2b_ref, x1a_ref, h2a_ref))


def _key_mask_features(ns, r, pos0, t, nrows):
    e = lax.broadcasted_iota(jnp.int32, (nrows, LANES), 0)
    f = lax.broadcasted_iota(jnp.int32, (nrows, LANES), 1) - HD
    if ns == 1:
        group = e // CHUNK
        before_start = e < jnp.maximum(0, WIN - (pos0 + t * r))
    else:
        nh = ns * WIN
        is_hist = e < nh
        group = jnp.where(is_hist, e // WIN, (e - nh) // r)
        before_start = is_hist & (e % WIN < WIN - pos0)
    n_groups = _n_key_groups(ns, nrows)
    return ((f == group) | ((f == n_groups) & before_start)).astype(F32).astype(BF)


def _query_mask_bias(ns, r, nrows):
    tm = ns * r
    c = lax.broadcasted_iota(jnp.int32, (HD, tm), 0)
    q = lax.broadcasted_iota(jnp.int32, (HD, tm), 1)
    n_groups = _n_key_groups(ns, nrows)
    if ns == 1:
        d = c - q // CHUNK
        hidden = (d < 0) | (d > WIN // CHUNK)
    else:
        hidden = c != q // r
    hidden = ((c < n_groups) & hidden) | (c == n_groups)
    return jnp.where(hidden, NEG_INF, 0.0).astype(BF)


def _n_key_groups(ns, nrows):
    n_groups = nrows // CHUNK if ns == 1 else ns
    assert n_groups + 1 <= HD
    return n_groups


def _softmax_t(k_blk, qt_blks, qbias, sinks):
    rhs = jnp.concatenate([jnp.concatenate(qt_blks, axis=1), jnp.concatenate([qbias] * GQA, axis=1)], axis=0)
    s = jnp.dot(k_blk, rhs, preferred_element_type=F32)
    ps, sink_terms = [], []
    for g in range(GQA):
        sg = s[:, g * LANES:(g + 1) * LANES]
        sink = jnp.full((1, LANES), sinks[g], F32)
        m = jnp.maximum(jnp.max(sg, axis=0, keepdims=True), sink)
        ps.append(jnp.exp(sg - m).astype(BF))
        sink_terms.append(jnp.exp(sink - m))
    return jnp.concatenate(ps, axis=1), sink_terms


def _attn_mixer_pieces(a, x1_ref, h2_ref, t, *, ns, r, pos0, n_tiles, layer, make_kv):
    tm = ns * r
    nh = ns * WIN
    _, blocks = _row_blocks(ns, r)
    x_ref, mod_ref, kext_ref, vtext_ref = a["x"], a["mod"], a["kext"], a["vtext"]
    hb_ref, qt_ref, attnt_ref = a["hb"], a["qt"], a["attnt"]

    def v_rows(j):
        return slice(j * (HD + V_AUG_ROWS), j * (HD + V_AUG_ROWS) + HD)

    def setup():
        kprev = a["khist"][...].astype(BF)
        vtprev = a["vthist"][...].astype(BF)
        if n_tiles > 1:
            kprev = jnp.where(t == 0, kprev, kext_ref[:, r:r + WIN, 0:HD])
            vtprev = jnp.where(t == 0, vtprev, jnp.concatenate(
                [vtext_ref[v_rows(j), r:r + WIN] for j in range(NKV)], axis=0))
        feats = _key_mask_features(ns, r, pos0, t, nh + tm)
        ones_row = (lax.broadcasted_iota(jnp.int32, (V_AUG_ROWS, nh + tm), 0) == 0).astype(F32).astype(BF)
        for j in range(NKV):
            kext_ref[j] = feats
            vtext_ref[j * (HD + V_AUG_ROWS) + HD:(j + 1) * (HD + V_AUG_ROWS), :] = ones_row
            vtext_ref[v_rows(j), 0:nh] = vtprev[j * HD:(j + 1) * HD, :]
        kext_ref[:, 0:nh, 0:HD] = kprev
        if not make_kv:
            kext_ref[:, nh:, 0:HD] = a["knew"][...]
            for j in range(NKV):
                vtext_ref[v_rows(j), nh:] = a["vtnew"][j * HD:(j + 1) * HD, :]
        a["qbias"][...] = _query_mask_bias(ns, r, nh + tm)

    def rows_in(rows):
        x = _load_rows(x_ref, rows, ns, r)
        xn = x * lax.rsqrt(jnp.mean(x * x, axis=-1, keepdims=True) + EPS)
        hb_ref[rows, :] = (xn * a["gmix"][0] * (1.0 + _mod(mod_ref, 1, ns, r)) + _mod(mod_ref, 0, ns, r)).astype(BF)
        if make_kv:
            modkv_ref = a["modkv"]
            a["hkv"][rows, :] = (
                xn * a["gkv"][...] * (1.0 + _mod(modkv_ref, 1, ns, r)) + _mod(modkv_ref, 0, ns, r)).astype(BF)

    def rows_kv(rows):
        kv = jnp.dot(a["hkv"][rows, :], a["wkv"][...], preferred_element_type=F32)
        v = kv[:, KVD:]
        ks = [_rms(kv[:, j * HD:(j + 1) * HD], a["gk"][...]) for j in range(NKV)]
        ext_rows = slice(nh + rows.start, nh + rows.stop)
        vt = v.T.astype(BF)
        for j in range(NKV):
            kj = ks[j].astype(BF)
            kext_ref[j, ext_rows, 0:HD] = kj
            a["kbf"][j, rows, :] = kj
            vtext_ref[v_rows(j), ext_rows] = vt[j * HD:(j + 1) * HD, :]
        a["vt"][:, rows] = vt
        tr = a["ktail"].shape[1]
        if ns > 1:
            for j in range(NKV):
                a["ktail"][:, :, j * HD:(j + 1) * HD] = ks[j].reshape(ns, r, HD)[:, r - tr:, :]
            a["vtail"][...] = v.reshape(ns, r, KVD)[:, r - tr:, :]
        elif rows.stop == r:
            for j in range(NKV):
                a["ktail"][0, :, j * HD:(j + 1) * HD] = ks[j]
            a["vtail"][0] = v

    def q_heads(j):
        qt = lax.dot_general(a["wqt"][0, j * GQA * HD:(j + 1) * GQA * HD, :], hb_ref[...], NT_DIMS,
                             preferred_element_type=F32)
        gq = jnp.concatenate([a["gq"][0]] * (tm // LANES), axis=1)
        for g in range(GQA):
            blk = qt[g * HD:(g + 1) * HD, :]
            inv = lax.rsqrt(jnp.mean(blk * blk, axis=0, keepdims=True) + EPS)
            hd = j * GQA + g
            qt_ref[hd * HD:(hd + 1) * HD, :] = (blk * inv * gq * (HD ** -0.5)).astype(BF)

    sink_terms = {}

    def scores(k, j, q0, k0, nk):
        heads = [j * GQA + g for g in range(GQA)]
        pt, sink_terms[k] = _softmax_t(
            kext_ref[j, k0:k0 + nk, :], [qt_ref[hd * HD:(hd + 1) * HD, q0:q0 + LANES] for hd in heads],
            a["qbias"][:, q0:q0 + LANES], [a["sinks"][layer, hd] for hd in heads])
        a["pt"][k % PT_SLOTS] = pt

    def values(k, j, q0, k0, nk):
        o = jnp.dot(vtext_ref[j * (HD + V_AUG_ROWS):(j + 1) * (HD + V_AUG_ROWS), k0:k0 + nk], a["pt"][k % PT_SLOTS],
                    preferred_element_type=F32)
        for g, sink_term in enumerate(sink_terms.pop(k)):
            hd = j * GQA + g
            lanes = slice(g * LANES, (g + 1) * LANES)
            inv = 1.0 / (o[HD:HD + 1, lanes] + sink_term)
            attnt_ref[hd * HD:(hd + 1) * HD, q0:q0 + LANES] = (o[0:HD, lanes] * inv).astype(BF)

    def rows_mix(rows):
        mix = lax.dot_general(attnt_ref[:, rows], a["wo"][0], TN_DIMS, preferred_element_type=F32)
        x1 = _load_rows(x_ref, rows, ns, r) + _mod(mod_ref, 2, ns, r) * mix
        x1_ref[rows, :] = x1
        h2_ref[rows, :] = (_rms(x1, a["gffn"][0]) * (1.0 + _mod(mod_ref, 4, ns, r)) + _mod(mod_ref, 3, ns, r)).astype(BF)

    if ns == 1:
        windows = [(q0, q0, WIN + LANES) for q0 in range(0, tm, LANES)]
    else:
        windows = [(0, 0, nh + tm)]
    pieces = [setup] + [functools.partial(rows_in, rows) for rows in blocks]
    if make_kv:
        pieces += [functools.partial(rows_kv, rows) for rows in blocks]
    pieces += [functools.partial(q_heads, j) for j in range(NKV)]
    att = [(k, j) + w for k, (j, w) in enumerate((j, w) for j in range(NKV) for w in windows)]
    pieces += _lagged([functools.partial(scores, *e) for e in att], [functools.partial(values, *e) for e in att],
                      PT_SLOTS - 1)
    return pieces + [functools.partial(rows_mix, rows) for rows in blocks]


def _attn_layer_kernel(*refs, names, ns, r, pos0, n_tiles, n_steps, layer, make_kv):
    a = dict(zip(names, refs))
    n = pl.program_id(0)
    t = lax.rem(jnp.minimum(n, n_steps - 1), n_tiles)

    def mixer(x1_ref, h2_ref):
        return _attn_mixer_pieces(a, x1_ref, h2_ref, t, ns=ns, r=r, pos0=pos0, n_tiles=n_tiles,
                                  layer=layer, make_kv=make_kv)

    def step(x1_cur, h2_cur, x1_nxt, h2_nxt):
        ffn = _ffn_pieces(x1_cur, h2_cur, a["modc"], a["win"], a["wout"], a["act"], a["o"], ns=ns, r=r)
        if n_steps == 1:
            _run(ffn)
        else:
            _interleave(ffn, mixer(x1_nxt, h2_nxt))

    _skewed_steps(n, lambda: _run(mixer(a["x1a"], a["h2a"])),
                  lambda: step(a["x1a"], a["h2a"], a["x1b"], a["h2b"]),
                  lambda: step(a["x1b"], a["h2b"], a["x1a"], a["h2a"]))


def _ada_kernel(c_ref, w_ref, b_ref, o_ref):
    c = c_ref[...]
    c_act = (c * _sigmoid(c)).astype(BF)
    o_ref[0] = jnp.dot(c_act, w_ref[0].astype(BF), preferred_element_type=F32) + b_ref[0]


def _ada_call(c_all, w, b):
    nl, _, n = w.shape
    m = c_all.shape[0]
    nb = next(c for c in ADA_COL_BLOCKS if n % c == 0)
    return pl.pallas_call(
        _ada_kernel,
        grid=(nl, n // nb),
        in_specs=[
            pl.BlockSpec((m, D), lambda l, j: (0, 0)),
            pl.BlockSpec((1, D, nb), lambda l, j: (l, 0, j)),
            pl.BlockSpec((1, 1, nb), lambda l, j: (l, 0, j)),
        ],
        out_specs=pl.BlockSpec((1, m, nb), lambda l, j: (l, 0, j)),
        out_shape=jax.ShapeDtypeStruct((nl, m, n), F32),
        compiler_params=pltpu.CompilerParams(
            dimension_semantics=("arbitrary", "arbitrary"), vmem_limit_bytes=V7X_VMEM_LIMIT_BYTES),
        name="ada",
    )(c_all, w, b.reshape(nl, 1, n))


class _Specs:
    def __init__(self, ns, r, n_tiles, n_steps):
        self.ns, self.r, self.n_tiles = ns, r, n_tiles
        self.mix = lambda n: jnp.minimum(n, n_steps - 1)
        self.ffn = lambda n: jnp.maximum(n - 1, 0)

    def tile(self, which, width):
        return pl.BlockSpec((self.ns, self.r, width),
                            lambda n: (which(n) // self.n_tiles, which(n) % self.n_tiles, 0))

    def group(self, which, rows, width):
        return pl.BlockSpec((self.ns, rows, width), lambda n: (which(n) // self.n_tiles, 0, 0))

    def k_rows(self, which, rows_per_group=None):
        if rows_per_group is None:
            return pl.BlockSpec((NKV, self.ns * self.r, HD), lambda n: (0, which(n), 0))
        return pl.BlockSpec((NKV, self.ns * rows_per_group, HD), lambda n: (0, which(n) // self.n_tiles, 0))

    def vt_cols(self, which, rows_per_group=None):
        if rows_per_group is None:
            return pl.BlockSpec((KVD, self.ns * self.r), lambda n: (0, which(n)))
        return pl.BlockSpec((KVD, self.ns * rows_per_group), lambda n: (0, which(n) // self.n_tiles))

    @staticmethod
    def const(shape):
        return pl.BlockSpec(shape, lambda n: (0,) * len(shape), pipeline_mode=pl.Buffered(1))

    @staticmethod
    def layer(shape, i):
        return pl.BlockSpec((1,) + shape, lambda n: (i,) + (0,) * len(shape), pipeline_mode=pl.Buffered(1))


def _layer_params():
    return pltpu.CompilerParams(dimension_semantics=("arbitrary",), vmem_limit_bytes=V7X_VMEM_LIMIT_BYTES)


def _pool_layer_call(x, mod, hist, wts, i, *, ns, r, pos0):
    bn, seq, _ = x.shape
    n_tiles = seq // r
    n_steps = (bn // ns) * n_tiles
    tm = ns * r
    sp = _Specs(ns, r, n_tiles, n_steps)
    return pl.pallas_call(
        functools.partial(_pool_layer_kernel, ns=ns, r=r, pos0=pos0, n_tiles=n_tiles, n_steps=n_steps),
        grid=(n_steps + 1,),
        in_specs=[
            sp.tile(sp.mix, D), sp.group(sp.mix, 6, D), sp.group(sp.ffn, 6, D), sp.group(sp.mix, HALO, D),
            sp.layer((1, D), i), sp.layer((1, D), i), sp.layer((len(POOL_WINDOWS), PGD, PGD), i),
            sp.layer((1, D), i), sp.layer((D, 2 * DFF), i), sp.layer((DFF, D), i),
        ],
        out_specs=[sp.tile(sp.ffn, D), sp.group(sp.mix, HALO, D)],
        out_shape=[jax.ShapeDtypeStruct((bn, seq, D), F32), jax.ShapeDtypeStruct((bn, HALO, D), F32)],
        scratch_shapes=[
            pltpu.VMEM((ns, r + HALO, D), F32), pltpu.VMEM((tm, D), BF),
            pltpu.VMEM((tm, D), F32), pltpu.VMEM((tm, D), F32),
            pltpu.VMEM((tm, D), BF), pltpu.VMEM((tm, D), BF),
            pltpu.VMEM((tm, DFF), BF),
        ],
        compiler_params=_layer_params(),
        name="pool_layer",
    )(x, mod, mod, hist, wts["g_mix"], wts["g_ffn"], wts["w_pool"], wts["pool_scale"],
      wts["w_ffn_in"], wts["w_ffn_out"])


def _attn_layer_call(x, mod, khist, vthist, kv_in, wts, i, j, *, ns, r, pos0):
    bn, seq, _ = x.shape
    n_tiles = seq // r
    n_steps = (bn // ns) * n_tiles
    tm = ns * r
    nh = ns * WIN
    tr = min(r, WIN)
    make_kv = len(kv_in) == 1
    assert ns == 1 or (n_tiles == 1 and r <= CHUNK)
    sp = _Specs(ns, r, n_tiles, n_steps)
    ins = [
        ("x", x, sp.tile(sp.mix, D)), ("mod", mod, sp.group(sp.mix, 6, D)), ("modc", mod, sp.group(sp.ffn, 6, D)),
        ("khist", khist, sp.k_rows(sp.mix, WIN)), ("vthist", vthist, sp.vt_cols(sp.mix, WIN)),
        ("sinks", wts["sinks"], pl.BlockSpec(memory_space=pltpu.SMEM)),
        ("gmix", wts["g_mix"], sp.layer((1, D), i)), ("gffn", wts["g_ffn"], sp.layer((1, D), i)),
        ("wqt", wts["w_qt"], sp.layer((D, D), j)), ("gq", wts["g_q"], sp.layer((HD, LANES), j)),
        ("wo", wts["w_o"], sp.layer((D, D), j)),
        ("win", wts["w_ffn_in"], sp.layer((D, 2 * DFF), i)), ("wout", wts["w_ffn_out"], sp.layer((DFF, D), i)),
    ]
    outs = [("o", jax.ShapeDtypeStruct((bn, seq, D), F32), sp.tile(sp.ffn, D))]
    if make_kv:
        ins += [("modkv", kv_in[0], sp.group(sp.mix, 2, D)), ("gkv", wts["g_kv"], sp.const((1, D))),
                ("wkv", wts["w_kv"], sp.const((D, 2 * KVD))), ("gk", wts["g_k"], sp.const((1, HD)))]
        outs += [
            ("ktail", jax.ShapeDtypeStruct((bn, tr, KVD), F32), sp.group(sp.mix, tr, KVD)),
            ("vtail", jax.ShapeDtypeStruct((bn, tr, KVD), F32), sp.group(sp.mix, tr, KVD)),
            ("kbf", jax.ShapeDtypeStruct((NKV, bn * seq, HD), BF), sp.k_rows(sp.mix)),
            ("vt", jax.ShapeDtypeStruct((KVD, bn * seq), BF), sp.vt_cols(sp.mix)),
        ]
    else:
        ins += [("knew", kv_in[0], sp.k_rows(sp.mix)), ("vtnew", kv_in[1], sp.vt_cols(sp.mix))]
    scratch = [
        ("kext", pltpu.VMEM((NKV, nh + tm, 2 * HD), BF)),
        ("vtext", pltpu.VMEM((NKV * (HD + V_AUG_ROWS), nh + tm), BF)), ("qbias", pltpu.VMEM((HD, tm), BF)),
        ("hb", pltpu.VMEM((tm, D), BF)), ("qt", pltpu.VMEM((D, tm), BF)), ("attnt", pltpu.VMEM((D, tm), BF)),
        ("pt", pltpu.VMEM((PT_SLOTS, WIN + LANES if ns == 1 else nh + tm, GQA * LANES), BF)),
        ("x1a", pltpu.VMEM((tm, D), F32)), ("x1b", pltpu.VMEM((tm, D), F32)),
        ("h2a", pltpu.VMEM((tm, D), BF)), ("h2b", pltpu.VMEM((tm, D), BF)),
        ("act", pltpu.VMEM((tm, DFF), BF)),
    ]
    if make_kv:
        scratch.append(("hkv", pltpu.VMEM((tm, D), BF)))
    names = [e[0] for e in ins] + [e[0] for e in outs] + [e[0] for e in scratch]
    return pl.pallas_call(
        functools.partial(_attn_layer_kernel, names=names, ns=ns, r=r, pos0=pos0, n_tiles=n_tiles,
                          n_steps=n_steps, layer=j, make_kv=make_kv),
        grid=(n_steps + 1,),
        in_specs=[e[2] for e in ins],
        out_specs=[e[2] for e in outs],
        out_shape=[e[1] for e in outs],
        scratch_shapes=[e[1] for e in scratch],
        compiler_params=_layer_params(),
        name="attn_layer_kv" if make_kv else "attn_layer",
    )(*[e[1] for e in ins])


def _trunk(x, mod, modkv, pool_hist, k_hist, v_hist, wts, *, ns, r, pos0):
    n_a = pool_hist.shape[0]
    bn = x.shape[0]
    tails = []
    for i in range(n_a):
        x, tail = _pool_layer_call(x, mod[i], pool_hist[i], wts, i, ns=ns, r=r, pos0=pos0)
        tails.append(tail[:, 1:, :])
    khist = k_hist.transpose(2, 0, 1, 3).reshape(NKV, bn * WIN, HD)
    vthist = v_hist.transpose(2, 3, 0, 1).reshape(KVD, bn * WIN)
    x, ktail, vtail, kbf, vt = _attn_layer_call(x, mod[n_a], khist, vthist, (modkv,), wts, n_a, 0,
                                                ns=ns, r=r, pos0=pos0)
    for j in range(1, wts["w_qt"].shape[0]):
        x, = _attn_layer_call(x, mod[n_a + j], khist, vthist, (kbf, vt), wts, n_a + j, j, ns=ns, r=r, pos0=pos0)
    k_tail = jnp.concatenate([k_hist.reshape(bn, WIN, KVD), ktail], axis=1)[:, -WIN:].reshape(bn, WIN, NKV, HD)
    v_tail = jnp.concatenate([v_hist.reshape(bn, WIN, KVD), vtail], axis=1)[:, -WIN:].reshape(bn, WIN, NKV, HD)
    return x, jnp.stack(tails, axis=0), k_tail, v_tail


def kernel(x_prompt, x_sample, c_prompt, c_sample, state_pool, cache_k, cache_v, w_ada, b_ada, g_mix, g_ffn, w_pool, pool_scale, w_q, g_q, sinks, w_o, g_kv, w_ada_kv, b_ada_kv, w_kv, g_k, w_ffn_in, w_ffn_out):
    depth = w_ada.shape[0]
    n_a = state_pool.shape[0]
    bp = x_prompt.shape[0]
    bs, ls, _ = x_sample.shape

    c_all = jnp.concatenate([c_prompt, c_sample, jnp.zeros((16 - bp - bs, D), F32)], axis=0)
    ada = _ada_call(c_all, w_ada, b_ada).reshape(depth, 16, 6, D)
    ada_kv = _ada_call(c_all, w_ada_kv[None], b_ada_kv[None]).reshape(16, 2, D)

    wts = dict(
        g_mix=g_mix.reshape(depth, 1, D), g_ffn=g_ffn.reshape(depth, 1, D),
        w_pool=w_pool.astype(BF), pool_scale=pool_scale.reshape(n_a, 1, D),
        w_qt=w_q.astype(BF).transpose(0, 2, 1),
        g_q=jnp.broadcast_to(g_q[:, :, None], g_q.shape + (LANES,)),
        sinks=sinks, w_o=w_o.astype(BF),
        g_kv=g_kv.reshape(1, D), w_kv=w_kv.astype(BF), g_k=g_k.reshape(1, HD),
        w_ffn_in=w_ffn_in.astype(BF), w_ffn_out=w_ffn_out.astype(BF),
    )

    zero_pool = jnp.zeros((n_a, bp, HALO, D), F32)
    zero_kv = jnp.zeros((bp, WIN, NKV, HD), F32)
    y_p, pool_p, k_p, v_p = _trunk(
        x_prompt, ada[:, :bp], ada_kv[:bp], zero_pool, zero_kv, zero_kv, wts,
        ns=1, r=PROMPT_TILE_ROWS, pos0=0)

    pool_hist_s = jnp.pad(state_pool, ((0, 0), (0, 0), (HALO - state_pool.shape[2], 0), (0, 0)))
    y_s, pool_s, k_s, v_s = _trunk(
        x_sample, ada[:, bp:bp + bs], ada_kv[bp:bp + bs], pool_hist_s, cache_k, cache_v, wts,
        ns=bs, r=ls, pos0=PAST_LEN)
    return (y_p, y_s, pool_p, k_p, v_p, pool_s, k_s, v_s)
```

```python
import functools

import jax
import jax.numpy as jnp
from jax import lax
from jax.experimental import pallas as pl
from jax.experimental.pallas import tpu as pltpu

D = 1024
DFF = 2816
HD = 64
NH = 16
NKV = 4
GQA = NH // NKV
KVD = NKV * HD
WIN = 128
CHUNK = 64
PAST_LEN = 4096
POOL_WINDOWS = (2, 4, 8, 16)
PGD = D // len(POOL_WINDOWS)
V_AUG_ROWS = 16
HALO = 16
LANES = 128
EPS = 1e-6
NEG_INF = -1e30
BF = jnp.bfloat16
F32 = jnp.float32

V7X_VMEM_LIMIT_BYTES = 60 * 1024 * 1024
TILE_ROWS_POOL = 256
TILE_ROWS_ATTN_KV = 256
TILE_ROWS_ATTN = 512
FFN_COL_CHUNK = 256
ROW_PIECE = 128
ATTN_ROW_PIECE = 256
FFN_OUT_ROWS = 512
PT_SLOTS = 3
ADA_COL_BLOCKS = (1536, 1024, 128)

NT_DIMS = (((1,), (1,)), ((), ()))
TN_DIMS = (((0,), (0,)), ((), ()))


def _rms(x, g):
    return x * lax.rsqrt(jnp.mean(x * x, axis=-1, keepdims=True) + EPS) * g


def _sigmoid(x):
    return 1.0 / (1.0 + jnp.exp(-x))


def _mod(mod_ref, k, ns, r):
    if ns == 1:
        return mod_ref[0, k:k + 1, :]
    return jnp.concatenate([jnp.broadcast_to(mod_ref[s, k:k + 1, :], (r, D)) for s in range(ns)], axis=0)


def _row_blocks(ns, r):
    tm = ns * r
    rb = min(tm, ATTN_ROW_PIECE)
    assert ns == 1 or rb == tm
    return rb, [slice(b * rb, (b + 1) * rb) for b in range(tm // rb)]


def _load_rows(ref, rows, ns, r):
    return ref[0, rows, :] if ns == 1 else ref[...].reshape(ns * r, ref.shape[-1])


def _store_rows(ref, rows, val, ns, r):
    if ns == 1:
        ref[0, rows, :] = val
    else:
        ref[...] = val.reshape(ns, r, ref.shape[-1])


def _run(pieces):
    for piece in pieces:
        piece()


def _interleave(main, side):
    merged = [((i + 0.5) / len(main), 0, f) for i, f in enumerate(main)]
    merged += [((i + 0.5) / len(side), 1, f) for i, f in enumerate(side)]
    _run(f for _, _, f in sorted(merged, key=lambda e: e[:2]))


def _skewed_steps(n, first, odd, even):
    pl.when(n == 0)(first)
    pl.when(lax.rem(n, 2) == 1)(odd)
    pl.when((n > 0) & (lax.rem(n, 2) == 0))(even)


def _ffn_pieces(x1_ref, h2_ref, modc_ref, win_ref, wout_ref, act_ref, o_ref, *, ns, r):
    tm = ns * r
    rb = min(tm, FFN_OUT_ROWS)
    blocks = [slice(b * rb, (b + 1) * rb) for b in range(tm // rb)]

    def chunk(c):
        lo = c * FFN_COL_CHUNK
        hi = lo + FFN_COL_CHUNK
        gate = jnp.dot(h2_ref[...], win_ref[0, :, lo:hi], preferred_element_type=F32)
        up = jnp.dot(h2_ref[...], win_ref[0, :, DFF + lo:DFF + hi], preferred_element_type=F32)
        act_ref[:, lo:hi] = (gate * _sigmoid(gate) * up).astype(BF)

    def rows_out(rows):
        y = jnp.dot(act_ref[rows, :], wout_ref[0], preferred_element_type=F32)
        g2 = _mod(modc_ref, 5, ns, r)
        _store_rows(o_ref, rows, x1_ref[rows, :] + g2 * y, ns, r)

    return ([functools.partial(chunk, c) for c in range(DFF // FFN_COL_CHUNK)]
            + [functools.partial(rows_out, rows) for rows in blocks])


def _lagged(firsts, seconds, lag):
    out = []
    for k in range(len(firsts) + lag):
        if k < len(firsts):
            out.append(firsts[k])
        if k >= lag:
            out.append(seconds[k - lag])
    return out


def _pool_mixer_pieces(x_ref, mod_ref, hist_ref, gmix_ref, gffn_ref, wpool_ref, pscale_ref,
                       tail_ref, halo_ref, diff_ref, x1_ref, h2_ref, t, *, ns, r, pos0, n_tiles):
    rp = min(r, ROW_PIECE)

    def pool(s, p):
        lo = p * rp
        rows = slice(s * r + lo, s * r + lo + rp)

        def m(k):
            return mod_ref[s, k:k + 1, :]

        if p > 0:
            halo = halo_ref[s]
        elif n_tiles == 1:
            halo = hist_ref[s]
        else:
            halo = jnp.where(t == 0, hist_ref[s], halo_ref[s])
        x = x_ref[s, lo:lo + rp, :]
        h = _rms(x, gmix_ref[0] * (1.0 + m(1))) + m(0)
        halo_ref[s] = h[rp - HALO:, :]
        if lo + rp == r:
            tail_ref[s] = h[rp - HALO:, :]

        pos = pos0 + t * r + lo + lax.broadcasted_iota(jnp.int32, (rp, 1), 0)
        for g, w in enumerate(POOL_WINDOWS):
            cols = slice(g * PGD, (g + 1) * PGD)
            acc = jnp.concatenate([halo[:, cols], h[:, cols]], axis=0)
            step = 1
            while step < w:
                acc = acc + pltpu.roll(acc, step, 0)
                step *= 2
            inv_count = 1.0 / jnp.minimum(w, pos + 1).astype(F32)
            diff_ref[rows, cols] = (acc[HALO:, :] * inv_count - h[:, cols]).astype(BF)

    def mix(s, p):
        lo = p * rp
        rows = slice(s * r + lo, s * r + lo + rp)

        def m(k):
            return mod_ref[s, k:k + 1, :]

        mixed = jnp.concatenate(
            [jnp.dot(diff_ref[rows, g * PGD:(g + 1) * PGD], wpool_ref[0, g], preferred_element_type=F32)
             for g in range(len(POOL_WINDOWS))], axis=1)
        x1 = x_ref[s, lo:lo + rp, :] + (m(2) * pscale_ref[0]) * mixed
        x1_ref[rows, :] = x1
        h2_ref[rows, :] = (_rms(x1, gffn_ref[0] * (1.0 + m(4))) + m(3)).astype(BF)

    sp = [(s, p) for s in range(ns) for p in range(r // rp)]
    return _lagged([functools.partial(pool, *e) for e in sp], [functools.partial(mix, *e) for e in sp], 1)


def _pool_layer_kernel(x_ref, mod_ref, modc_ref, hist_ref, gmix_ref, gffn_ref, wpool_ref, pscale_ref,
                       win_ref, wout_ref, o_ref, tail_ref,
                       halo_ref, diff_ref, x1a_ref, x1b_ref, h2a_ref, h2b_ref, act_ref,
                       *, ns, r, pos0, n_tiles, n_steps):
    n = pl.program_id(0)
    t = lax.rem(jnp.minimum(n, n_steps - 1), n_tiles)

    def mixer(x1_ref, h2_ref):
        return _pool_mixer_pieces(x_ref, mod_ref, hist_ref, gmix_ref, gffn_ref, wpool_ref, pscale_ref,
                                  tail_ref, halo_ref, diff_ref, x1_ref, h2_ref, t,
                                  ns=ns, r=r, pos0=pos0, n_tiles=n_tiles)

    def step(x1_cur, h2_cur, x1_nxt, h2_nxt):
        ffn = _ffn_pieces(x1_cur, h2_cur, modc_ref, win_ref, wout_ref, act_ref, o_ref, ns=ns, r=r)
        if n_steps == 1:
            _run(ffn)
        else:
            _interleave(ffn, mixer(x1_nxt, h2_nxt))

    _skewed_steps(n, lambda: _run(mixer(x1a_ref, h2a_ref)),
                  lambda: step(x1a_ref, h2a_ref, x1b_ref, h2b_ref),
                  lambda: step(x1b_ref, h2b_ref, x1a_ref, h2a_ref))


def _key_mask_features(ns, r, pos0, t, nrows):
    e = lax.broadcasted_iota(jnp.int32, (nrows, LANES), 0)
    f = lax.broadcasted_iota(jnp.int32, (nrows, LANES), 1) - HD
    if ns == 1:
        group = e // CHUNK
        before_start = e < jnp.maximum(0, WIN - (pos0 + t * r))
    else:
        nh = ns * WIN
        is_hist = e < nh
        group = jnp.where(is_hist, e // WIN, (e - nh) // r)
        before_start = is_hist & (e % WIN < WIN - pos0)
    n_groups = _n_key_groups(ns, nrows)
    return ((f == group) | ((f == n_groups) & before_start)).astype(F32).astype(BF)


def _query_mask_bias(ns, r, nrows):
    tm = ns * r
    c = lax.broadcasted_iota(jnp.int32, (HD, tm), 0)
    q = lax.broadcasted_iota(jnp.int32, (HD, tm), 1)
    n_groups = _n_key_groups(ns, nrows)
    if ns == 1:
        d = c - q // CHUNK
        hidden = (d < 0) | (d > WIN // CHUNK)
    else:
        hidden = c != q // r
    hidden = ((c < n_groups) & hidden) | (c == n_groups)
    return jnp.where(hidden, NEG_INF, 0.0).astype(BF)


def _n_key_groups(ns, nrows):
    n_groups = nrows // CHUNK if ns == 1 else ns
    assert n_groups + 1 <= HD
    return n_groups


def _softmax_t(k_blk, qt_blks, qbias, sinks):
    rhs = jnp.concatenate([jnp.concatenate(qt_blks, axis=1), jnp.concatenate([qbias] * GQA, axis=1)], axis=0)
    s = jnp.dot(k_blk, rhs, preferred_element_type=F32)
    ps, sink_terms = [], []
    for g in range(GQA):
        sg = s[:, g * LANES:(g + 1) * LANES]
        sink = jnp.full((1, LANES), sinks[g], F32)
        m = jnp.maximum(jnp.max(sg, axis=0, keepdims=True), sink)
        ps.append(jnp.exp(sg - m).astype(BF))
        sink_terms.append(jnp.exp(sink - m))
    return jnp.concatenate(ps, axis=1), sink_terms


def _attn_mixer_pieces(a, x1_ref, h2_ref, t, *, ns, r, pos0, n_tiles, layer, make_kv):
    tm = ns * r
    nh = ns * WIN
    _, blocks = _row_blocks(ns, r)
    x_ref, mod_ref, kext_ref, vtext_ref = a["x"], a["mod"], a["kext"], a["vtext"]
    hb_ref, qt_ref, attnt_ref = a["hb"], a["qt"], a["attnt"]

    def v_rows(j):
        return slice(j * (HD + V_AUG_ROWS), j * (HD + V_AUG_ROWS) + HD)

    def setup():
        kprev = a["khist"][...].astype(BF)
        vtprev = a["vthist"][...].astype(BF)
        if n_tiles > 1:
            kprev = jnp.where(t == 0, kprev, kext_ref[:, r:r + WIN, 0:HD])
            vtprev = jnp.where(t == 0, vtprev, jnp.concatenate(
                [vtext_ref[v_rows(j), r:r + WIN] for j in range(NKV)], axis=0))
        feats = _key_mask_features(ns, r, pos0, t, nh + tm)
        ones_row = (lax.broadcasted_iota(jnp.int32, (V_AUG_ROWS, nh + tm), 0) == 0).astype(F32).astype(BF)
        for j in range(NKV):
            kext_ref[j] = feats
            vtext_ref[j * (HD + V_AUG_ROWS) + HD:(j + 1) * (HD + V_AUG_ROWS), :] = ones_row
            vtext_ref[v_rows(j), 0:nh] = vtprev[j * HD:(j + 1) * HD, :]
        kext_ref[:, 0:nh, 0:HD] = kprev
        if not make_kv:
            kext_ref[:, nh:, 0:HD] = a["knew"][...]
            for j in range(NKV):
                vtext_ref[v_rows(j), nh:] = a["vtnew"][j * HD:(j + 1) * HD, :]
        a["qbias"][...] = _query_mask_bias(ns, r, nh + tm)

    def rows_in(rows):
        x = _load_rows(x_ref, rows, ns, r)
        xn = x * lax.rsqrt(jnp.mean(x * x, axis=-1, keepdims=True) + EPS)
        hb_ref[rows, :] = (xn * (a["gmix"][0] * (1.0 + _mod(mod_ref, 1, ns, r))) + _mod(mod_ref, 0, ns, r)).astype(BF)
        if make_kv:
            modkv_ref = a["modkv"]
            a["hkv"][rows, :] = (
                xn * (a["gkv"][...] * (1.0 + _mod(modkv_ref, 1, ns, r))) + _mod(modkv_ref, 0, ns, r)).astype(BF)

    def rows_kv(rows):
        kv = jnp.dot(a["hkv"][rows, :], a["wkv"][...], preferred_element_type=F32)
        v = kv[:, KVD:]
        ks = [_rms(kv[:, j * HD:(j + 1) * HD], a["gk"][...]) for j in range(NKV)]
        ext_rows = slice(nh + rows.start, nh + rows.stop)
        vt = v.T.astype(BF)
        for j in range(NKV):
            kj = ks[j].astype(BF)
            kext_ref[j, ext_rows, 0:HD] = kj
            a["kbf"][j, rows, :] = kj
            vtext_ref[v_rows(j), ext_rows] = vt[j * HD:(j + 1) * HD, :]
        a["vt"][:, rows] = vt
        tr = a["ktail"].shape[1]
        if ns > 1:
            for j in range(NKV):
                a["ktail"][:, :, j * HD:(j + 1) * HD] = ks[j].reshape(ns, r, HD)[:, r - tr:, :]
            a["vtail"][...] = v.reshape(ns, r, KVD)[:, r - tr:, :]
        elif rows.stop == r:
            n = rows.stop - rows.start
            for j in range(NKV):
                a["ktail"][0, :, j * HD:(j + 1) * HD] = ks[j][n - tr:, :]
            a["vtail"][0] = v[n - tr:, :]

    def q_heads(j):
        qt = lax.dot_general(a["wqt"][0, j * GQA * HD:(j + 1) * GQA * HD, :], hb_ref[...], NT_DIMS,
                             preferred_element_type=F32)
        gq = jnp.concatenate([a["gq"][0] * (HD ** -0.5)] * (tm // LANES), axis=1)
        for g in range(GQA):
            blk = qt[g * HD:(g + 1) * HD, :]
            inv = lax.rsqrt(jnp.mean(blk * blk, axis=0, keepdims=True) + EPS)
            hd = j * GQA + g
            qt_ref[hd * HD:(hd + 1) * HD, :] = (blk * inv * gq).astype(BF)

    sink_terms = {}

    def scores(k, j, q0, k0, nk):
        heads = [j * GQA + g for g in range(GQA)]
        pt, sink_terms[k] = _softmax_t(
            kext_ref[j, k0:k0 + nk, :], [qt_ref[hd * HD:(hd + 1) * HD, q0:q0 + LANES] for hd in heads],
            a["qbias"][:, q0:q0 + LANES], [a["sinks"][layer, hd] for hd in heads])
        a["pt"][k % PT_SLOTS] = pt

    def values(k, j, q0, k0, nk):
        o = jnp.dot(vtext_ref[j * (HD + V_AUG_ROWS):(j + 1) * (HD + V_AUG_ROWS), k0:k0 + nk], a["pt"][k % PT_SLOTS],
                    preferred_element_type=F32)
        for g, sink_term in enumerate(sink_terms.pop(k)):
            hd = j * GQA + g
            lanes = slice(g * LANES, (g + 1) * LANES)
            inv = 1.0 / (o[HD:HD + 1, lanes] + sink_term)
            attnt_ref[hd * HD:(hd + 1) * HD, q0:q0 + LANES] = (o[0:HD, lanes] * inv).astype(BF)

    def rows_mix(rows):
        mix = lax.dot_general(attnt_ref[:, rows], a["wo"][0], TN_DIMS, preferred_element_type=F32)
        x1 = _load_rows(x_ref, rows, ns, r) + _mod(mod_ref, 2, ns, r) * mix
        x1_ref[rows, :] = x1
        h2_ref[rows, :] = (_rms(x1, a["gffn"][0] * (1.0 + _mod(mod_ref, 4, ns, r))) + _mod(mod_ref, 3, ns, r)).astype(BF)

    if ns == 1:
        windows = [(q0, q0, WIN + LANES) for q0 in range(0, tm, LANES)]
    else:
        windows = [(0, 0, nh + tm)]
    pieces = [setup] + [functools.partial(rows_in, rows) for rows in blocks]
    if make_kv:
        pieces += [functools.partial(rows_kv, rows) for rows in blocks]
    pieces += [functools.partial(q_heads, j) for j in range(NKV)]
    att = [(k, j) + w for k, (j, w) in enumerate((j, w) for j in range(NKV) for w in windows)]
    pieces += _lagged([functools.partial(scores, *e) for e in att], [functools.partial(values, *e) for e in att],
                      PT_SLOTS - 1)
    return pieces + [functools.partial(rows_mix, rows) for rows in blocks]


def _attn_layer_kernel(*refs, names, ns, r, pos0, n_tiles, n_steps, layer, make_kv):
    a = dict(zip(names, refs))
    n = pl.program_id(0)
    t = lax.rem(jnp.minimum(n, n_steps - 1), n_tiles)

    def mixer(x1_ref, h2_ref):
        return _attn_mixer_pieces(a, x1_ref, h2_ref, t, ns=ns, r=r, pos0=pos0, n_tiles=n_tiles,
                                  layer=layer, make_kv=make_kv)

    def step(x1_cur, h2_cur, x1_nxt, h2_nxt):
        ffn = _ffn_pieces(x1_cur, h2_cur, a["modc"], a["win"], a["wout"], a["act"], a["o"], ns=ns, r=r)
        if n_steps == 1:
            _run(ffn)
        else:
            _interleave(ffn, mixer(x1_nxt, h2_nxt))

    _skewed_steps(n, lambda: _run(mixer(a["x1a"], a["h2a"])),
                  lambda: step(a["x1a"], a["h2a"], a["x1b"], a["h2b"]),
                  lambda: step(a["x1b"], a["h2b"], a["x1a"], a["h2a"]))


def _ada_kernel(c_ref, w_ref, b_ref, o_ref):
    c = c_ref[...]
    c_act = (c * _sigmoid(c)).astype(BF)
    o_ref[0] = jnp.dot(c_act, w_ref[0].astype(BF), preferred_element_type=F32) + b_ref[0]


def _ada_call(c_all, w, b):
    nl, _, n = w.shape
    m = c_all.shape[0]
    nb = next(c for c in ADA_COL_BLOCKS if n % c == 0)
    return pl.pallas_call(
        _ada_kernel,
        grid=(nl, n // nb),
        in_specs=[
            pl.BlockSpec((m, D), lambda l, j: (0, 0)),
            pl.BlockSpec((1, D, nb), lambda l, j: (l, 0, j)),
            pl.BlockSpec((1, 1, nb), lambda l, j: (l, 0, j)),
        ],
        out_specs=pl.BlockSpec((1, m, nb), lambda l, j: (l, 0, j)),
        out_shape=jax.ShapeDtypeStruct((nl, m, n), F32),
        compiler_params=pltpu.CompilerParams(
            dimension_semantics=("arbitrary", "arbitrary"), vmem_limit_bytes=V7X_VMEM_LIMIT_BYTES),
        name="ada",
    )(c_all, w, b.reshape(nl, 1, n))


class _Specs:
    def __init__(self, ns, r, n_tiles, n_steps):
        self.ns, self.r, self.n_tiles = ns, r, n_tiles
        self.mix = lambda n: jnp.minimum(n, n_steps - 1)
        self.ffn = lambda n: jnp.maximum(n - 1, 0)

    def tile(self, which, width):
        return pl.BlockSpec((self.ns, self.r, width),
                            lambda n: (which(n) // self.n_tiles, which(n) % self.n_tiles, 0))

    def group(self, which, rows, width):
        return pl.BlockSpec((self.ns, rows, width), lambda n: (which(n) // self.n_tiles, 0, 0))

    def k_rows(self, which, rows_per_group=None):
        if rows_per_group is None:
            return pl.BlockSpec((NKV, self.ns * self.r, HD), lambda n: (0, which(n), 0))
        return pl.BlockSpec((NKV, self.ns * rows_per_group, HD), lambda n: (0, which(n) // self.n_tiles, 0))

    def vt_cols(self, which, rows_per_group=None):
        if rows_per_group is None:
            return pl.BlockSpec((KVD, self.ns * self.r), lambda n: (0, which(n)))
        return pl.BlockSpec((KVD, self.ns * rows_per_group), lambda n: (0, which(n) // self.n_tiles))

    @staticmethod
    def const(shape):
        return pl.BlockSpec(shape, lambda n: (0,) * len(shape), pipeline_mode=pl.Buffered(1))

    @staticmethod
    def layer(shape, i):
        return pl.BlockSpec((1,) + shape, lambda n: (i,) + (0,) * len(shape), pipeline_mode=pl.Buffered(1))


def _layer_params():
    return pltpu.CompilerParams(dimension_semantics=("arbitrary",), vmem_limit_bytes=V7X_VMEM_LIMIT_BYTES)


def _pool_layer_call(x, mod, hist, wts, i, *, ns, r, pos0):
    bn, seq, _ = x.shape
    n_tiles = seq // r
    n_steps = (bn // ns) * n_tiles
    tm = ns * r
    sp = _Specs(ns, r, n_tiles, n_steps)
    return pl.pallas_call(
        functools.partial(_pool_layer_kernel, ns=ns, r=r, pos0=pos0, n_tiles=n_tiles, n_steps=n_steps),
        grid=(n_steps + 1,),
        in_specs=[
            sp.tile(sp.mix, D), sp.group(sp.mix, 6, D), sp.group(sp.ffn, 6, D), sp.group(sp.mix, HALO, D),
            sp.layer((1, D), i), sp.layer((1, D), i), sp.layer((len(POOL_WINDOWS), PGD, PGD), i),
            sp.layer((1, D), i), sp.layer((D, 2 * DFF), i), sp.layer((DFF, D), i),
        ],
        out_specs=[sp.tile(sp.ffn, D), sp.group(sp.mix, HALO, D)],
        out_shape=[jax.ShapeDtypeStruct((bn, seq, D), F32), jax.ShapeDtypeStruct((bn, HALO, D), F32)],
        scratch_shapes=[
            pltpu.VMEM((ns, HALO, D), F32), pltpu.VMEM((tm, D), BF),
            pltpu.VMEM((tm, D), F32), pltpu.VMEM((tm, D), F32),
            pltpu.VMEM((tm, D), BF), pltpu.VMEM((tm, D), BF),
            pltpu.VMEM((tm, DFF), BF),
        ],
        compiler_params=_layer_params(),
        name="pool_layer",
    )(x, mod, mod, hist, wts["g_mix"], wts["g_ffn"], wts["w_pool"], wts["pool_scale"],
      wts["w_ffn_in"], wts["w_ffn_out"])


def _attn_layer_call(x, mod, khist, vthist, kv_in, wts, i, j, *, ns, r, pos0):
    bn, seq, _ = x.shape
    n_tiles = seq // r
    n_steps = (bn // ns) * n_tiles
    tm = ns * r
    nh = ns * WIN
    tr = min(r, WIN)
    make_kv = len(kv_in) == 1
    assert ns == 1 or (n_tiles == 1 and r <= CHUNK)
    sp = _Specs(ns, r, n_tiles, n_steps)
    ins = [
        ("x", x, sp.tile(sp.mix, D)), ("mod", mod, sp.group(sp.mix, 6, D)), ("modc", mod, sp.group(sp.ffn, 6, D)),
        ("khist", khist, sp.k_rows(sp.mix, WIN)), ("vthist", vthist, sp.vt_cols(sp.mix, WIN)),
        ("sinks", wts["sinks"], pl.BlockSpec(memory_space=pltpu.SMEM)),
        ("gmix", wts["g_mix"], sp.layer((1, D), i)), ("gffn", wts["g_ffn"], sp.layer((1, D), i)),
        ("wqt", wts["w_qt"], sp.layer((D, D), j)), ("gq", wts["g_q"], sp.layer((HD, LANES), j)),
        ("wo", wts["w_o"], sp.layer((D, D), j)),
        ("win", wts["w_ffn_in"], sp.layer((D, 2 * DFF), i)), ("wout", wts["w_ffn_out"], sp.layer((DFF, D), i)),
    ]
    outs = [("o", jax.ShapeDtypeStruct((bn, seq, D), F32), sp.tile(sp.ffn, D))]
    if make_kv:
        ins += [("modkv", kv_in[0], sp.group(sp.mix, 2, D)), ("gkv", wts["g_kv"], sp.const((1, D))),
                ("wkv", wts["w_kv"], sp.const((D, 2 * KVD))), ("gk", wts["g_k"], sp.const((1, HD)))]
        outs += [
            ("ktail", jax.ShapeDtypeStruct((bn, tr, KVD), F32), sp.group(sp.mix, tr, KVD)),
            ("vtail", jax.ShapeDtypeStruct((bn, tr, KVD), F32), sp.group(sp.mix, tr, KVD)),
            ("kbf", jax.ShapeDtypeStruct((NKV, bn * seq, HD), BF), sp.k_rows(sp.mix)),
            ("vt", jax.ShapeDtypeStruct((KVD, bn * seq), BF), sp.vt_cols(sp.mix)),
        ]
    else:
        ins += [("knew", kv_in[0], sp.k_rows(sp.mix)), ("vtnew", kv_in[1], sp.vt_cols(sp.mix))]
    scratch = [
        ("kext", pltpu.VMEM((NKV, nh + tm, 2 * HD), BF)),
        ("vtext", pltpu.VMEM((NKV * (HD + V_AUG_ROWS), nh + tm), BF)), ("qbias", pltpu.VMEM((HD, tm), BF)),
        ("hb", pltpu.VMEM((tm, D), BF)), ("qt", pltpu.VMEM((D, tm), BF)), ("attnt", pltpu.VMEM((D, tm), BF)),
        ("pt", pltpu.VMEM((PT_SLOTS, WIN + LANES if ns == 1 else nh + tm, GQA * LANES), BF)),
        ("x1a", pltpu.VMEM((tm, D), F32)), ("x1b", pltpu.VMEM((tm, D), F32)),
        ("h2a", pltpu.VMEM((tm, D), BF)), ("h2b", pltpu.VMEM((tm, D), BF)),
        ("act", pltpu.VMEM((tm, DFF), BF)),
    ]
    if make_kv:
        scratch.append(("hkv", pltpu.VMEM((tm, D), BF)))
    names = [e[0] for e in ins] + [e[0] for e in outs] + [e[0] for e in scratch]
    return pl.pallas_call(
        functools.partial(_attn_layer_kernel, names=names, ns=ns, r=r, pos0=pos0, n_tiles=n_tiles,
                          n_steps=n_steps, layer=j, make_kv=make_kv),
        grid=(n_steps + 1,),
        in_specs=[e[2] for e in ins],
        out_specs=[e[2] for e in outs],
        out_shape=[e[1] for e in outs],
        scratch_shapes=[e[1] for e in scratch],
        compiler_params=_layer_params(),
        name="attn_layer_kv" if make_kv else "attn_layer",
    )(*[e[1] for e in ins])


def _trunk(x, mod, modkv, pool_hist, k_hist, v_hist, wts, *, ns, r, pos0):
    n_a = pool_hist.shape[0]
    bn, seq, _ = x.shape
    tails = []
    for i in range(n_a):
        x, tail = _pool_layer_call(x, mod[i], pool_hist[i], wts, i, ns=ns, r=min(r, TILE_ROWS_POOL), pos0=pos0)
        tails.append(tail[:, 1:, :])
    khist = k_hist.transpose(2, 0, 1, 3).reshape(NKV, bn * WIN, HD)
    vthist = v_hist.transpose(2, 3, 0, 1).reshape(KVD, bn * WIN)
    x, ktail, vtail, kbf, vt = _attn_layer_call(x, mod[n_a], khist, vthist, (modkv,), wts, n_a, 0,
                                                ns=ns, r=min(r, TILE_ROWS_ATTN_KV), pos0=pos0)
    for j in range(1, wts["w_qt"].shape[0]):
        x, = _attn_layer_call(x, mod[n_a + j], khist, vthist, (kbf, vt), wts, n_a + j, j,
                              ns=ns, r=min(r, TILE_ROWS_ATTN), pos0=pos0)
    k_tail = jnp.concatenate([k_hist.reshape(bn, WIN, KVD), ktail], axis=1)[:, -WIN:].reshape(bn, WIN, NKV, HD)
    v_tail = jnp.concatenate([v_hist.reshape(bn, WIN, KVD), vtail], axis=1)[:, -WIN:].reshape(bn, WIN, NKV, HD)
    return x, jnp.stack(tails, axis=0), k_tail, v_tail


def kernel(x_prompt, x_sample, c_prompt, c_sample, state_pool, cache_k, cache_v, w_ada, b_ada, g_mix, g_ffn, w_pool, pool_scale, w_q, g_q, sinks, w_o, g_kv, w_ada_kv, b_ada_kv, w_kv, g_k, w_ffn_in, w_ffn_out):
    depth = w_ada.shape[0]
    n_a = state_pool.shape[0]
    bp = x_prompt.shape[0]
    bs, ls, _ = x_sample.shape

    c_all = jnp.concatenate([c_prompt, c_sample, jnp.zeros((16 - bp - bs, D), F32)], axis=0)
    ada = _ada_call(c_all, w_ada, b_ada).reshape(depth, 16, 6, D)
    ada_kv = _ada_call(c_all, w_ada_kv[None], b_ada_kv[None]).reshape(16, 2, D)

    wts = dict(
        g_mix=g_mix.reshape(depth, 1, D), g_ffn=g_ffn.reshape(depth, 1, D),
        w_pool=w_pool.astype(BF), pool_scale=pool_scale.reshape(n_a, 1, D),
        w_qt=w_q.astype(BF).transpose(0, 2, 1),
        g_q=jnp.broadcast_to(g_q[:, :, None], g_q.shape + (LANES,)),
        sinks=sinks, w_o=w_o.astype(BF),
        g_kv=g_kv.reshape(1, D), w_kv=w_kv.astype(BF), g_k=g_k.reshape(1, HD),
        w_ffn_in=w_ffn_in.astype(BF), w_ffn_out=w_ffn_out.astype(BF),
    )

    zero_pool = jnp.zeros((n_a, bp, HALO, D), F32)
    zero_kv = jnp.zeros((bp, WIN, NKV, HD), F32)
    y_p, pool_p, k_p, v_p = _trunk(
        x_prompt, ada[:, :bp], ada_kv[:bp], zero_pool, zero_kv, zero_kv, wts,
        ns=1, r=x_prompt.shape[1], pos0=0)

    pool_hist_s = jnp.pad(state_pool, ((0, 0), (0, 0), (HALO - state_pool.shape[2], 0), (0, 0)))
    y_s, pool_s, k_s, v_s = _trunk(
        x_sample, ada[:, bp:bp + bs], ada_kv[bp:bp + bs], pool_hist_s, cache_k, cache_v, wts,
        ns=bs, r=ls, pos0=PAST_LEN)
    return (y_p, y_s, pool_p, k_p, v_p, pool_s, k_s, v_s)
```

```python
import functools

import jax
import jax.numpy as jnp
from jax import lax
from jax.experimental import pallas as pl
from jax.experimental.pallas import tpu as pltpu

D = 1024
DFF = 2816
HD = 64
NH = 16
NKV = 4
GQA = NH // NKV
KVD = NKV * HD
WIN = 128
CHUNK = 64
PAST_LEN = 4096
POOL_WINDOWS = (2, 4, 8, 16)
PGD = D // len(POOL_WINDOWS)
V_AUG_ROWS = 16
HALO = 16
LANES = 128
EPS = 1e-6
NEG_INF = -1e30
BF = jnp.bfloat16
F32 = jnp.float32

V7X_VMEM_LIMIT_BYTES = 60 * 1024 * 1024
TILE_ROWS_POOL = 256
TILE_ROWS_ATTN_KV = 256
TILE_ROWS_ATTN = 256
FFN_COL_CHUNK = 256
ROW_PIECE = 128
ATTN_ROW_PIECE = 256
FFN_OUT_ROWS = 512
PT_SLOTS = 3
WEIGHT_CHUNKS = 8
ADA_COL_BLOCKS = (1536, 1024, 128)

NT_DIMS = (((1,), (1,)), ((), ()))
TN_DIMS = (((0,), (0,)), ((), ()))


def _rms(x, g):
    return x * lax.rsqrt(jnp.mean(x * x, axis=-1, keepdims=True) + EPS) * g


def _sigmoid(x):
    return 1.0 / (1.0 + jnp.exp(-x))


def _mod(mod_ref, k, ns, r):
    if ns == 1:
        return mod_ref[0, k:k + 1, :]
    return jnp.concatenate([jnp.broadcast_to(mod_ref[s, k:k + 1, :], (r, D)) for s in range(ns)], axis=0)


def _row_blocks(ns, r):
    tm = ns * r
    rb = min(tm, ATTN_ROW_PIECE)
    assert ns == 1 or rb == tm
    return [slice(b * rb, (b + 1) * rb) for b in range(tm // rb)]


def _load_rows(ref, rows, ns, r):
    return ref[0, rows, :] if ns == 1 else ref[...].reshape(ns * r, ref.shape[-1])


def _store_rows(ref, rows, val, ns, r):
    if ns == 1:
        ref[0, rows, :] = val
    else:
        ref[...] = val.reshape(ns, r, ref.shape[-1])


def _run(pieces):
    for piece in pieces:
        piece()


def _lagged(firsts, seconds, lag):
    out = []
    for k in range(len(firsts) + lag):
        if k < len(firsts):
            out.append(firsts[k])
        if k >= lag:
            out.append(seconds[k - lag])
    return out


def _load_weight_chunks(n, pairs, transposed=()):
    def load(c):
        for src, dst in pairs:
            chunk = src[...].reshape(src.shape[-2:])
            rows = chunk.shape[0]
            if any(dst is ref for ref in transposed):
                dst[:, c * rows:(c + 1) * rows] = chunk.T.astype(BF)
            else:
                dst[c * rows:(c + 1) * rows, :] = chunk.astype(BF)

    for c in range(WEIGHT_CHUNKS):
        pl.when(n == c)(functools.partial(load, c))


def _ffn_pieces(x1_ref, h2_ref, mod_ref, win_ref, wout_ref, act_ref, o_ref, *, ns, r):
    tm = ns * r
    rb = min(tm, FFN_OUT_ROWS)
    blocks = [slice(b * rb, (b + 1) * rb) for b in range(tm // rb)]

    def chunk(c):
        lo = c * FFN_COL_CHUNK
        hi = lo + FFN_COL_CHUNK
        gate = jnp.dot(h2_ref[...], win_ref[:, lo:hi], preferred_element_type=F32)
        up = jnp.dot(h2_ref[...], win_ref[:, DFF + lo:DFF + hi], preferred_element_type=F32)
        act_ref[:, lo:hi] = (gate * _sigmoid(gate) * up).astype(BF)

    def rows_out(rows):
        y = jnp.dot(act_ref[rows, :], wout_ref[...], preferred_element_type=F32)
        _store_rows(o_ref, rows, x1_ref[rows, :] + _mod(mod_ref, 5, ns, r) * y, ns, r)

    return ([functools.partial(chunk, c) for c in range(DFF // FFN_COL_CHUNK)]
            + [functools.partial(rows_out, rows) for rows in blocks])


def _pool_mixer_pieces(a, t, *, ns, r, pos0, n_tiles):
    rp = min(r, ROW_PIECE)
    x_ref, mod_ref, halo_ref, diff_ref = a["x"], a["mod"], a["halo"], a["diff"]

    def pool(s, p):
        lo = p * rp
        rows = slice(s * r + lo, s * r + lo + rp)

        def m(k):
            return mod_ref[s, k:k + 1, :]

        if p > 0:
            halo = halo_ref[s]
        elif n_tiles == 1:
            halo = a["hist"][s]
        else:
            halo = jnp.where(t == 0, a["hist"][s], halo_ref[s])
        x = x_ref[s, lo:lo + rp, :]
        h = _rms(x, a["gmix"][0] * (1.0 + m(1))) + m(0)
        halo_ref[s] = h[rp - HALO:, :]
        if lo + rp == r:
            a["tail"][s] = h[rp - HALO:, :]

        pos = pos0 + t * r + lo + lax.broadcasted_iota(jnp.int32, (rp, 1), 0)
        for g, w in enumerate(POOL_WINDOWS):
            cols = slice(g * PGD, (g + 1) * PGD)
            acc = jnp.concatenate([halo[:, cols], h[:, cols]], axis=0)
            step = 1
            while step < w:
                acc = acc + pltpu.roll(acc, step, 0)
                step *= 2
            inv_count = 1.0 / jnp.minimum(w, pos + 1).astype(F32)
            diff_ref[rows, cols] = (acc[HALO:, :] * inv_count - h[:, cols]).astype(BF)

    def mix(s, p):
        lo = p * rp
        rows = slice(s * r + lo, s * r + lo + rp)

        def m(k):
            return mod_ref[s, k:k + 1, :]

        mixed = jnp.concatenate(
            [jnp.dot(diff_ref[rows, g * PGD:(g + 1) * PGD], a["wpool_bf"][g], preferred_element_type=F32)
             for g in range(len(POOL_WINDOWS))], axis=1)
        x1 = x_ref[s, lo:lo + rp, :] + (m(2) * a["pscale"][0]) * mixed
        a["x1"][rows, :] = x1
        a["h2"][rows, :] = (_rms(x1, a["gffn"][0] * (1.0 + m(4))) + m(3)).astype(BF)

    sp = [(s, p) for s in range(ns) for p in range(r // rp)]
    return _lagged([functools.partial(pool, *e) for e in sp], [functools.partial(mix, *e) for e in sp], 1)


def _key_mask_features(ns, r, pos0, t, nrows):
    e = lax.broadcasted_iota(jnp.int32, (nrows, LANES), 0)
    f = lax.broadcasted_iota(jnp.int32, (nrows, LANES), 1) - HD
    if ns == 1:
        group = e // CHUNK
        before_start = e < jnp.maximum(0, WIN - (pos0 + t * r))
    else:
        nh = ns * WIN
        is_hist = e < nh
        group = jnp.where(is_hist, e // WIN, (e - nh) // r)
        before_start = is_hist & (e % WIN < WIN - pos0)
    n_groups = _n_key_groups(ns, nrows)
    return ((f == group) | ((f == n_groups) & before_start)).astype(F32).astype(BF)


def _query_mask_bias(ns, r, nrows):
    tm = ns * r
    c = lax.broadcasted_iota(jnp.int32, (HD, tm), 0)
    q = lax.broadcasted_iota(jnp.int32, (HD, tm), 1)
    n_groups = _n_key_groups(ns, nrows)
    if ns == 1:
        d = c - q // CHUNK
        hidden = (d < 0) | (d > WIN // CHUNK)
    else:
        hidden = c != q // r
    hidden = ((c < n_groups) & hidden) | (c == n_groups)
    return jnp.where(hidden, NEG_INF, 0.0).astype(BF)


def _n_key_groups(ns, nrows):
    n_groups = nrows // CHUNK if ns == 1 else ns
    assert n_groups + 1 <= HD
    return n_groups


def _softmax_t(k_blk, qt_blks, qbias, sinks):
    rhs = jnp.concatenate([jnp.concatenate(qt_blks, axis=1), jnp.concatenate([qbias] * GQA, axis=1)], axis=0)
    s = jnp.dot(k_blk, rhs, preferred_element_type=F32)
    ps, sink_terms = [], []
    for g in range(GQA):
        sg = s[:, g * LANES:(g + 1) * LANES]
        sink = jnp.full((1, LANES), sinks[g], F32)
        m = jnp.maximum(jnp.max(sg, axis=0, keepdims=True), sink)
        ps.append(jnp.exp(sg - m).astype(BF))
        sink_terms.append(jnp.exp(sink - m))
    return jnp.concatenate(ps, axis=1), sink_terms


def _attn_mixer_pieces(a, t, *, ns, r, pos0, n_tiles, layer, make_kv):
    tm = ns * r
    nh = ns * WIN
    blocks = _row_blocks(ns, r)
    x_ref, mod_ref, kext_ref, vtext_ref = a["x"], a["mod"], a["kext"], a["vtext"]
    hb_ref, qt_ref, attnt_ref = a["hb"], a["qt"], a["attnt"]

    def v_rows(j):
        return slice(j * (HD + V_AUG_ROWS), j * (HD + V_AUG_ROWS) + HD)

    def setup():
        kprev = a["khist"][...].astype(BF)
        vtprev = a["vthist"][...].astype(BF)
        if n_tiles > 1:
            kprev = jnp.where(t == 0, kprev, kext_ref[:, r:r + WIN, 0:HD])
            vtprev = jnp.where(t == 0, vtprev, jnp.concatenate(
                [vtext_ref[v_rows(j), r:r + WIN] for j in range(NKV)], axis=0))
        feats = _key_mask_features(ns, r, pos0, t, nh + tm)
        ones_row = (lax.broadcasted_iota(jnp.int32, (V_AUG_ROWS, nh + tm), 0) == 0).astype(F32).astype(BF)
        for j in range(NKV):
            kext_ref[j] = feats
            vtext_ref[j * (HD + V_AUG_ROWS) + HD:(j + 1) * (HD + V_AUG_ROWS), :] = ones_row
            vtext_ref[v_rows(j), 0:nh] = vtprev[j * HD:(j + 1) * HD, :]
        kext_ref[:, 0:nh, 0:HD] = kprev
        if not make_kv:
            kext_ref[:, nh:, 0:HD] = a["knew"][...]
            for j in range(NKV):
                vtext_ref[v_rows(j), nh:] = a["vtnew"][j * HD:(j + 1) * HD, :]
        a["qbias"][...] = _query_mask_bias(ns, r, nh + tm)

    def rows_in(rows):
        x = _load_rows(x_ref, rows, ns, r)
        xn = x * lax.rsqrt(jnp.mean(x * x, axis=-1, keepdims=True) + EPS)
        hb_ref[rows, :] = (xn * (a["gmix"][0] * (1.0 + _mod(mod_ref, 1, ns, r))) + _mod(mod_ref, 0, ns, r)).astype(BF)
        if make_kv:
            modkv_ref = a["modkv"]
            a["hkv"][rows, :] = (
                xn * (a["gkv"][...] * (1.0 + _mod(modkv_ref, 1, ns, r))) + _mod(modkv_ref, 0, ns, r)).astype(BF)

    def rows_kv(rows):
        kv = jnp.dot(a["hkv"][rows, :], a["wkv_bf"][...], preferred_element_type=F32)
        v = kv[:, KVD:]
        ks = [_rms(kv[:, j * HD:(j + 1) * HD], a["gk"][...]) for j in range(NKV)]
        ext_rows = slice(nh + rows.start, nh + rows.stop)
        vt = v.T.astype(BF)
        for j in range(NKV):
            kj = ks[j].astype(BF)
            kext_ref[j, ext_rows, 0:HD] = kj
            a["kbf"][j, rows, :] = kj
            vtext_ref[v_rows(j), ext_rows] = vt[j * HD:(j + 1) * HD, :]
        a["vt"][:, rows] = vt
        tr = a["ktail"].shape[1]
        if ns > 1:
            for j in range(NKV):
                a["ktail"][:, :, j * HD:(j + 1) * HD] = ks[j].reshape(ns, r, HD)[:, r - tr:, :]
            a["vtail"][...] = v.reshape(ns, r, KVD)[:, r - tr:, :]
        elif rows.stop == r:
            n = rows.stop - rows.start
            for j in range(NKV):
                a["ktail"][0, :, j * HD:(j + 1) * HD] = ks[j][n - tr:, :]
            a["vtail"][0] = v[n - tr:, :]

    def q_heads(j):
        qt = lax.dot_general(a["wqt_bf"][j * GQA * HD:(j + 1) * GQA * HD, :], hb_ref[...], NT_DIMS,
                             preferred_element_type=F32)
        gq = jnp.concatenate([a["gq"][0] * (HD ** -0.5)] * (tm // LANES), axis=1)
        for g in range(GQA):
            blk = qt[g * HD:(g + 1) * HD, :]
            inv = lax.rsqrt(jnp.mean(blk * blk, axis=0, keepdims=True) + EPS)
            hd = j * GQA + g
            qt_ref[hd * HD:(hd + 1) * HD, :] = (blk * inv * gq).astype(BF)

    sink_terms = {}

    def scores(k, j, q0, k0, nk):
        heads = [j * GQA + g for g in range(GQA)]
        pt, sink_terms[k] = _softmax_t(
            kext_ref[j, k0:k0 + nk, :], [qt_ref[hd * HD:(hd + 1) * HD, q0:q0 + LANES] for hd in heads],
            a["qbias"][:, q0:q0 + LANES], [a["sinks"][layer, hd] for hd in heads])
        a["pt"][k % PT_SLOTS] = pt

    def values(k, j, q0, k0, nk):
        o = jnp.dot(vtext_ref[j * (HD + V_AUG_ROWS):(j + 1) * (HD + V_AUG_ROWS), k0:k0 + nk], a["pt"][k % PT_SLOTS],
                    preferred_element_type=F32)
        for g, sink_term in enumerate(sink_terms.pop(k)):
            hd = j * GQA + g
            lanes = slice(g * LANES, (g + 1) * LANES)
            inv = 1.0 / (o[HD:HD + 1, lanes] + sink_term)
            attnt_ref[hd * HD:(hd + 1) * HD, q0:q0 + LANES] = (o[0:HD, lanes] * inv).astype(BF)

    def rows_mix(rows):
        mix = lax.dot_general(attnt_ref[:, rows], a["wo_bf"][...], TN_DIMS, preferred_element_type=F32)
        x1 = _load_rows(x_ref, rows, ns, r) + _mod(mod_ref, 2, ns, r) * mix
        a["x1"][rows, :] = x1
        a["h2"][rows, :] = (_rms(x1, a["gffn"][0] * (1.0 + _mod(mod_ref, 4, ns, r))) + _mod(mod_ref, 3, ns, r)).astype(BF)

    if ns == 1:
        windows = [(q0, q0, WIN + LANES) for q0 in range(0, tm, LANES)]
    else:
        windows = [(0, 0, nh + tm)]
    pieces = [setup] + [functools.partial(rows_in, rows) for rows in blocks]
    if make_kv:
        pieces += [functools.partial(rows_kv, rows) for rows in blocks]
    pieces += [functools.partial(q_heads, j) for j in range(NKV)]
    att = [(k, j) + w for k, (j, w) in enumerate((j, w) for j in range(NKV) for w in windows)]
    pieces += _lagged([functools.partial(scores, *e) for e in att], [functools.partial(values, *e) for e in att],
                      PT_SLOTS - 1)
    return pieces + [functools.partial(rows_mix, rows) for rows in blocks]


def _layer_kernel(*refs, names, kind, streams, layer):
    refs = dict(zip(names, refs))
    n = pl.program_id(0)
    weights = [(refs["win"], refs["win_bf"]), (refs["wout"], refs["wout_bf"])]
    if kind != "pool":
        weights += [(refs["wq"], refs["wqt_bf"]), (refs["wo"], refs["wo_bf"])]
        if kind == "attn_kv":
            weights.append((refs["wkv"], refs["wkv_bf"]))
    _load_weight_chunks(n, weights, transposed=[refs.get("wqt_bf")])
    if kind == "pool":
        @pl.when(n == 0)
        def _():
            refs["wpool_bf"][...] = refs["wpool"][0].astype(BF)

    for sfx, ns, r, pos0, n_tiles, first, n_steps in streams:
        a = dict(refs)
        a.update({k[:-len(sfx)]: v for k, v in refs.items() if sfx and k.endswith(sfx)})

        def tile(a=a, ns=ns, r=r, pos0=pos0, n_tiles=n_tiles, first=first):
            t = lax.rem(n - first, n_tiles)
            if kind == "pool":
                _run(_pool_mixer_pieces(a, t, ns=ns, r=r, pos0=pos0, n_tiles=n_tiles))
            else:
                _run(_attn_mixer_pieces(a, t, ns=ns, r=r, pos0=pos0, n_tiles=n_tiles, layer=layer,
                                        make_kv=kind == "attn_kv"))
            _run(_ffn_pieces(a["x1"], a["h2"], a["mod"], a["win_bf"], a["wout_bf"], a["act"], a["o"], ns=ns, r=r))

        pl.when((n >= first) & (n < first + n_steps))(tile)


def _ada_kernel(c_ref, w_ref, b_ref, o_ref):
    c = c_ref[...]
    c_act = (c * _sigmoid(c)).astype(BF)
    o_ref[0] = jnp.dot(c_act, w_ref[0].astype(BF), preferred_element_type=F32) + b_ref[0]


def _ada_call(c_all, w, b):
    nl, _, n = w.shape
    m = c_all.shape[0]
    nb = next(c for c in ADA_COL_BLOCKS if n % c == 0)
    return pl.pallas_call(
        _ada_kernel,
        grid=(nl, n // nb),
        in_specs=[
            pl.BlockSpec((m, D), lambda l, j: (0, 0)),
            pl.BlockSpec((1, D, nb), lambda l, j: (l, 0, j)),
            pl.BlockSpec((1, 1, nb), lambda l, j: (l, 0, j)),
        ],
        out_specs=pl.BlockSpec((1, m, nb), lambda l, j: (l, 0, j)),
        out_shape=jax.ShapeDtypeStruct((nl, m, n), F32),
        compiler_params=pltpu.CompilerParams(
            dimension_semantics=("arbitrary", "arbitrary"), vmem_limit_bytes=V7X_VMEM_LIMIT_BYTES),
        name="ada",
    )(c_all, w, b.reshape(nl, 1, n))


class _Specs:
    def __init__(self, ns, r, n_tiles, n_steps, first):
        self.ns, self.r, self.n_tiles = ns, r, n_tiles
        self.tile_of = lambda n: jnp.clip(n - first, 0, n_steps - 1)

    def tile(self, width):
        return pl.BlockSpec((self.ns, self.r, width),
                            lambda n: (self.tile_of(n) // self.n_tiles, self.tile_of(n) % self.n_tiles, 0))

    def group(self, rows, width):
        return pl.BlockSpec((self.ns, rows, width), lambda n: (self.tile_of(n) // self.n_tiles, 0, 0))

    def k_rows(self, rows_per_group=None):
        if rows_per_group is None:
            return pl.BlockSpec((NKV, self.ns * self.r, HD), lambda n: (0, self.tile_of(n), 0))
        return pl.BlockSpec((NKV, self.ns * rows_per_group, HD), lambda n: (0, self.tile_of(n) // self.n_tiles, 0))

    def vt_cols(self, rows_per_group=None):
        if rows_per_group is None:
            return pl.BlockSpec((KVD, self.ns * self.r), lambda n: (0, self.tile_of(n)))
        return pl.BlockSpec((KVD, self.ns * rows_per_group), lambda n: (0, self.tile_of(n) // self.n_tiles))

    @staticmethod
    def const(shape):
        return pl.BlockSpec(shape, lambda n: (0,) * len(shape), pipeline_mode=pl.Buffered(1))

    @staticmethod
    def layer(shape, i):
        return pl.BlockSpec((1,) + shape, lambda n: (i,) + (0,) * len(shape), pipeline_mode=pl.Buffered(1))

    @staticmethod
    def weight_chunk(rows, cols, i=None):
        assert rows % WEIGHT_CHUNKS == 0
        chunk = rows // WEIGHT_CHUNKS
        if i is None:
            return pl.BlockSpec((chunk, cols), lambda n: (jnp.minimum(n, WEIGHT_CHUNKS - 1), 0))
        return pl.BlockSpec((1, chunk, cols), lambda n: (i, jnp.minimum(n, WEIGHT_CHUNKS - 1), 0))


def _layer_call(kind, streams, wts, i, j=None):
    ins = [
        ("gmix", wts["g_mix"], _Specs.layer((1, D), i)), ("gffn", wts["g_ffn"], _Specs.layer((1, D), i)),
        ("win", wts["w_ffn_in"], _Specs.weight_chunk(D, 2 * DFF, i)),
        ("wout", wts["w_ffn_out"], _Specs.weight_chunk(DFF, D, i)),
    ]
    outs = []
    scratch = [("win_bf", pltpu.VMEM((D, 2 * DFF), BF)), ("wout_bf", pltpu.VMEM((DFF, D), BF))]
    if kind == "pool":
        ins += [("wpool", wts["w_pool"], _Specs.layer((len(POOL_WINDOWS), PGD, PGD), i)),
                ("pscale", wts["pool_scale"], _Specs.layer((1, D), i))]
        scratch += [("wpool_bf", pltpu.VMEM((len(POOL_WINDOWS), PGD, PGD), BF))]
    else:
        ins += [("sinks", wts["sinks"], pl.BlockSpec(memory_space=pltpu.SMEM)),
                ("gq", wts["g_q"], _Specs.layer((HD, LANES), j)),
                ("wq", wts["w_q"], _Specs.weight_chunk(D, D, j)), ("wo", wts["w_o"], _Specs.weight_chunk(D, D, j))]
        scratch += [("wqt_bf", pltpu.VMEM((D, D), BF)), ("wo_bf", pltpu.VMEM((D, D), BF))]
        if kind == "attn_kv":
            ins += [("gkv", wts["g_kv"], _Specs.const((1, D))), ("wkv", wts["w_kv"], _Specs.weight_chunk(D, 2 * KVD)),
                    ("gk", wts["g_k"], _Specs.const((1, HD)))]
            scratch += [("wkv_bf", pltpu.VMEM((D, 2 * KVD), BF))]

    first = WEIGHT_CHUNKS
    stream_args, n_outs = [], []
    for k, st in enumerate(streams):
        sfx = "" if k == 0 else "_s%d" % k
        x, ns, r = st["x"], st["ns"], st["r"]
        bn, seq, _ = x.shape
        n_tiles = seq // r
        n_steps = (bn // ns) * n_tiles
        tm = ns * r
        nh = ns * WIN
        tr = min(r, WIN)
        assert ns == 1 or (n_tiles == 1 and r <= CHUNK)
        sp = _Specs(ns, r, n_tiles, n_steps, first)
        s_ins = [("x", x, sp.tile(D)), ("mod", st["mod"], sp.group(6, D))]
        s_outs = [("o", jax.ShapeDtypeStruct((bn, seq, D), F32), sp.tile(D))]
        s_scratch = [("x1", pltpu.VMEM((tm, D), F32)), ("h2", pltpu.VMEM((tm, D), BF)),
                     ("act", pltpu.VMEM((tm, DFF), BF))]
        if kind == "pool":
            s_ins += [("hist", st["pool_hist"], sp.group(HALO, D))]
            s_outs += [("tail", jax.ShapeDtypeStruct((bn, HALO, D), F32), sp.group(HALO, D))]
            s_scratch += [("halo", pltpu.VMEM((ns, HALO, D), F32)), ("diff", pltpu.VMEM((tm, D), BF))]
        else:
            s_ins += [("khist", st["khist"], sp.k_rows(WIN)), ("vthist", st["vthist"], sp.vt_cols(WIN))]
            s_scratch += [
                ("kext", pltpu.VMEM((NKV, nh + tm, 2 * HD), BF)),
                ("vtext", pltpu.VMEM((NKV * (HD + V_AUG_ROWS), nh + tm), BF)), ("qbias", pltpu.VMEM((HD, tm), BF)),
                ("hb", pltpu.VMEM((tm, D), BF)), ("qt", pltpu.VMEM((D, tm), BF)), ("attnt", pltpu.VMEM((D, tm), BF)),
                ("pt", pltpu.VMEM((PT_SLOTS, WIN + LANES if ns == 1 else nh + tm, GQA * LANES), BF)),
            ]
            if kind == "attn_kv":
                s_ins += [("modkv", st["modkv"], sp.group(2, D))]
                s_outs += [
                    ("ktail", jax.ShapeDtypeStruct((bn, tr, KVD), F32), sp.group(tr, KVD)),
                    ("vtail", jax.ShapeDtypeStruct((bn, tr, KVD), F32), sp.group(tr, KVD)),
                    ("kbf", jax.ShapeDtypeStruct((NKV, bn * seq, HD), BF), sp.k_rows()),
                    ("vt", jax.ShapeDtypeStruct((KVD, bn * seq), BF), sp.vt_cols()),
                ]
                s_scratch += [("hkv", pltpu.VMEM((tm, D), BF))]
            else:
                s_ins += [("knew", st["kbf"], sp.k_rows()), ("vtnew", st["vt"], sp.vt_cols())]
        ins += [(e[0] + sfx,) + e[1:] for e in s_ins]
        outs += [(e[0] + sfx,) + e[1:] for e in s_outs]
        scratch += [(e[0] + sfx,) + e[1:] for e in s_scratch]
        stream_args.append((sfx, ns, r, st["pos0"], n_tiles, first, n_steps))
        n_outs.append(len(s_outs))
        first += n_steps

    names = [e[0] for e in ins] + [e[0] for e in outs] + [e[0] for e in scratch]
    flat = pl.pallas_call(
        functools.partial(_layer_kernel, names=names, kind=kind, streams=tuple(stream_args), layer=j),
        grid=(first,),
        in_specs=[e[2] for e in ins],
        out_specs=[e[2] for e in outs],
        out_shape=[e[1] for e in outs],
        scratch_shapes=[e[1] for e in scratch],
        compiler_params=pltpu.CompilerParams(
            dimension_semantics=("arbitrary",), vmem_limit_bytes=V7X_VMEM_LIMIT_BYTES),
        name=kind + "_layer",
    )(*[e[1] for e in ins])
    results, at = [], 0
    for count in n_outs:
        results.append(list(flat[at:at + count]))
        at += count
    return results


def _trunks(streams, wts, n_a, n_b):
    xs = [st["x"] for st in streams]
    tails = [[] for _ in streams]

    def layer_streams(i, tile_rows, **extra):
        return [dict(x=xs[k], mod=st["mod"][i], ns=st["ns"], r=min(st["seq_rows"], tile_rows), pos0=st["pos0"],
                     **{name: vals[k] for name, vals in extra.items()}) for k, st in enumerate(streams)]

    for i in range(n_a):
        res = _layer_call("pool", layer_streams(i, TILE_ROWS_POOL, pool_hist=[st["pool_hist"][i] for st in streams]),
                          wts, i)
        for k, (x, tail) in enumerate(res):
            xs[k] = x
            tails[k].append(tail[:, 1:, :])
    khist = [st["k_hist"].transpose(2, 0, 1, 3).reshape(NKV, -1, HD) for st in streams]
    vthist = [st["v_hist"].transpose(2, 3, 0, 1).reshape(KVD, -1) for st in streams]
    res = _layer_call("attn_kv", layer_streams(n_a, TILE_ROWS_ATTN_KV, khist=khist, vthist=vthist,
                                               modkv=[st["modkv"] for st in streams]), wts, n_a, 0)
    xs = [r[0] for r in res]
    kv = [r[1:] for r in res]
    for j in range(1, n_b):
        res = _layer_call("attn", layer_streams(n_a + j, TILE_ROWS_ATTN, khist=khist, vthist=vthist,
                                                kbf=[e[2] for e in kv], vt=[e[3] for e in kv]), wts, n_a + j, j)
        xs = [r[0] for r in res]
    out = []
    for k, st in enumerate(streams):
        bn = xs[k].shape[0]
        k_tail = jnp.concatenate([st["k_hist"].reshape(bn, WIN, KVD), kv[k][0]], axis=1)[:, -WIN:]
        v_tail = jnp.concatenate([st["v_hist"].reshape(bn, WIN, KVD), kv[k][1]], axis=1)[:, -WIN:]
        out.append((xs[k], jnp.stack(tails[k], axis=0), k_tail.reshape(bn, WIN, NKV, HD),
                    v_tail.reshape(bn, WIN, NKV, HD)))
    return out


def kernel(x_prompt, x_sample, c_prompt, c_sample, state_pool, cache_k, cache_v, w_ada, b_ada, g_mix, g_ffn, w_pool, pool_scale, w_q, g_q, sinks, w_o, g_kv, w_ada_kv, b_ada_kv, w_kv, g_k, w_ffn_in, w_ffn_out):
    depth = w_ada.shape[0]
    n_a = state_pool.shape[0]
    bp = x_prompt.shape[0]
    bs, ls, _ = x_sample.shape

    c_all = jnp.concatenate([c_prompt, c_sample, jnp.zeros((16 - bp - bs, D), F32)], axis=0)
    ada = _ada_call(c_all, w_ada, b_ada).reshape(depth, 16, 6, D)
    ada_kv = _ada_call(c_all, w_ada_kv[None], b_ada_kv[None]).reshape(16, 2, D)

    wts = dict(
        g_mix=g_mix.reshape(depth, 1, D), g_ffn=g_ffn.reshape(depth, 1, D),
        w_pool=w_pool, pool_scale=pool_scale.reshape(n_a, 1, D),
        w_q=w_q, g_q=jnp.broadcast_to(g_q[:, :, None], g_q.shape + (LANES,)), sinks=sinks, w_o=w_o,
        g_kv=g_kv.reshape(1, D), w_kv=w_kv, g_k=g_k.reshape(1, HD),
        w_ffn_in=w_ffn_in, w_ffn_out=w_ffn_out,
    )

    zero_kv = jnp.zeros((bp, WIN, NKV, HD), F32)
    prompt = dict(x=x_prompt, mod=ada[:, :bp], modkv=ada_kv[:bp], pool_hist=jnp.zeros((n_a, bp, HALO, D), F32),
                  k_hist=zero_kv, v_hist=zero_kv, ns=1, seq_rows=x_prompt.shape[1], pos0=0)
    sample = dict(x=x_sample, mod=ada[:, bp:bp + bs], modkv=ada_kv[bp:bp + bs],
                  pool_hist=jnp.pad(state_pool, ((0, 0), (0, 0), (HALO - state_pool.shape[2], 0), (0, 0))),
                  k_hist=cache_k, v_hist=cache_v, ns=bs, seq_rows=ls, pos0=PAST_LEN)
    (y_p, pool_p, k_p, v_p), (y_s, pool_s, k_s, v_s) = _trunks([prompt, sample], wts, n_a, depth - n_a)
    return (y_p, y_s, pool_p, k_p, v_p, pool_s, k_s, v_s)
```

```python
import functools

import jax
import jax.numpy as jnp
from jax import lax
from jax.experimental import pallas as pl
from jax.experimental.pallas import tpu as pltpu

D = 1024
DFF = 2816
HD = 64
NH = 16
NKV = 4
GQA = NH // NKV
KVD = NKV * HD
WIN = 128
CHUNK = 64
PAST_LEN = 4096
POOL_WINDOWS = (2, 4, 8, 16)
PGD = D // len(POOL_WINDOWS)
V_AUG_ROWS = 16
HALO = 16
LANES = 128
EPS = 1e-6
NEG_INF = -1e30
BF = jnp.bfloat16
F32 = jnp.float32

V7X_VMEM_LIMIT_BYTES = 60 * 1024 * 1024
TILE_ROWS_POOL = 512
TILE_ROWS_ATTN_KV = 256
TILE_ROWS_ATTN = 512
FFN_COL_CHUNK = 256
ROW_PIECE = 128
ATTN_ROW_PIECE = 256
FFN_OUT_ROWS = 512
PT_SLOTS = 3
WEIGHT_CHUNKS = {"pool": 8, "attn_kv": 8, "attn": 16}
ADA_COL_BLOCKS = (1536, 1024, 128)

NT_DIMS = (((1,), (1,)), ((), ()))
TN_DIMS = (((0,), (0,)), ((), ()))


def _rms(x, g):
    return x * lax.rsqrt(jnp.mean(x * x, axis=-1, keepdims=True) + EPS) * g


def _sigmoid(x):
    return 1.0 / (1.0 + jnp.exp(-x))


def _mod(mod_ref, k, ns, r):
    if ns == 1:
        return mod_ref[0, k:k + 1, :]
    return jnp.concatenate([jnp.broadcast_to(mod_ref[s, k:k + 1, :], (r, D)) for s in range(ns)], axis=0)


def _row_blocks(ns, r):
    tm = ns * r
    rb = min(tm, ATTN_ROW_PIECE)
    assert ns == 1 or rb == tm
    return [slice(b * rb, (b + 1) * rb) for b in range(tm // rb)]


def _load_rows(ref, rows, ns, r):
    return ref[0, rows, :] if ns == 1 else ref[...].reshape(ns * r, ref.shape[-1])


def _store_rows(ref, rows, val, ns, r):
    if ns == 1:
        ref[0, rows, :] = val
    else:
        ref[...] = val.reshape(ns, r, ref.shape[-1])


def _run(pieces):
    for piece in pieces:
        piece()


def _lagged(firsts, seconds, lag):
    out = []
    for k in range(len(firsts) + lag):
        if k < len(firsts):
            out.append(firsts[k])
        if k >= lag:
            out.append(seconds[k - lag])
    return out


def _load_weight_chunks(n, n_chunks, pairs, transposed=()):
    def load(c):
        for src, dst in pairs:
            chunk = src[...].reshape(src.shape[-2:])
            rows = chunk.shape[0]
            if any(dst is ref for ref in transposed):
                dst[:, c * rows:(c + 1) * rows] = chunk.T.astype(BF)
            else:
                dst[c * rows:(c + 1) * rows, :] = chunk.astype(BF)

    for c in range(n_chunks):
        pl.when(n == c)(functools.partial(load, c))


def _ffn_pieces(x1_ref, h2_ref, mod_ref, win_ref, wout_ref, act_ref, o_ref, *, ns, r):
    tm = ns * r
    rb = min(tm, FFN_OUT_ROWS)
    blocks = [slice(b * rb, (b + 1) * rb) for b in range(tm // rb)]

    def chunk(c):
        lo = c * FFN_COL_CHUNK
        hi = lo + FFN_COL_CHUNK
        gate = jnp.dot(h2_ref[...], win_ref[:, lo:hi], preferred_element_type=F32)
        up = jnp.dot(h2_ref[...], win_ref[:, DFF + lo:DFF + hi], preferred_element_type=F32)
        act_ref[:, lo:hi] = (gate * _sigmoid(gate) * up).astype(BF)

    def rows_out(rows):
        y = jnp.dot(act_ref[rows, :], wout_ref[...], preferred_element_type=F32)
        _store_rows(o_ref, rows, x1_ref[rows, :] + _mod(mod_ref, 5, ns, r) * y, ns, r)

    return ([functools.partial(chunk, c) for c in range(DFF // FFN_COL_CHUNK)]
            + [functools.partial(rows_out, rows) for rows in blocks])


def _pool_mixer_pieces(a, t, *, ns, r, pos0, n_tiles):
    rp = min(r, ROW_PIECE)
    x_ref, mod_ref, halo_ref, diff_ref = a["x"], a["mod"], a["halo"], a["diff"]

    def pool(s, p):
        lo = p * rp
        rows = slice(s * r + lo, s * r + lo + rp)

        def m(k):
            return mod_ref[s, k:k + 1, :]

        if p > 0:
            halo = halo_ref[s]
        elif n_tiles == 1:
            halo = a["hist"][s]
        else:
            halo = jnp.where(t == 0, a["hist"][s], halo_ref[s])
        x = x_ref[s, lo:lo + rp, :]
        h = _rms(x, a["gmix"][0] * (1.0 + m(1))) + m(0)
        halo_ref[s] = h[rp - HALO:, :]
        if lo + rp == r:
            a["tail"][s] = h[rp - HALO:, :]

        pos = pos0 + t * r + lo + lax.broadcasted_iota(jnp.int32, (rp, 1), 0)
        for g, w in enumerate(POOL_WINDOWS):
            cols = slice(g * PGD, (g + 1) * PGD)
            acc = jnp.concatenate([halo[:, cols], h[:, cols]], axis=0)
            step = 1
            while step < w:
                acc = acc + pltpu.roll(acc, step, 0)
                step *= 2
            inv_count = 1.0 / jnp.minimum(w, pos + 1).astype(F32)
            diff_ref[rows, cols] = (acc[HALO:, :] * inv_count - h[:, cols]).astype(BF)

    def mix(s, p):
        lo = p * rp
        rows = slice(s * r + lo, s * r + lo + rp)

        def m(k):
            return mod_ref[s, k:k + 1, :]

        mixed = jnp.concatenate(
            [jnp.dot(diff_ref[rows, g * PGD:(g + 1) * PGD], a["wpool_bf"][g], preferred_element_type=F32)
             for g in range(len(POOL_WINDOWS))], axis=1)
        x1 = x_ref[s, lo:lo + rp, :] + (m(2) * a["pscale"][0]) * mixed
        a["x1"][rows, :] = x1
        a["h2"][rows, :] = (_rms(x1, a["gffn"][0] * (1.0 + m(4))) + m(3)).astype(BF)

    sp = [(s, p) for s in range(ns) for p in range(r // rp)]
    return _lagged([functools.partial(pool, *e) for e in sp], [functools.partial(mix, *e) for e in sp], 1)


def _key_mask_features(ns, r, pos0, t, nrows):
    e = lax.broadcasted_iota(jnp.int32, (nrows, LANES), 0)
    f = lax.broadcasted_iota(jnp.int32, (nrows, LANES), 1) - HD
    if ns == 1:
        group = e // CHUNK
        before_start = e < jnp.maximum(0, WIN - (pos0 + t * r))
    else:
        nh = ns * WIN
        is_hist = e < nh
        group = jnp.where(is_hist, e // WIN, (e - nh) // r)
        before_start = is_hist & (e % WIN < WIN - pos0)
    n_groups = _n_key_groups(ns, nrows)
    return ((f == group) | ((f == n_groups) & before_start)).astype(F32).astype(BF)


def _query_mask_bias(ns, r, nrows):
    tm = ns * r
    c = lax.broadcasted_iota(jnp.int32, (HD, tm), 0)
    q = lax.broadcasted_iota(jnp.int32, (HD, tm), 1)
    n_groups = _n_key_groups(ns, nrows)
    if ns == 1:
        d = c - q // CHUNK
        hidden = (d < 0) | (d > WIN // CHUNK)
    else:
        hidden = c != q // r
    hidden = ((c < n_groups) & hidden) | (c == n_groups)
    return jnp.where(hidden, NEG_INF, 0.0).astype(BF)


def _n_key_groups(ns, nrows):
    n_groups = nrows // CHUNK if ns == 1 else ns
    assert n_groups + 1 <= HD
    return n_groups


def _softmax_t(k_blk, qt_blks, qbias, sinks):
    rhs = jnp.concatenate([jnp.concatenate(qt_blks, axis=1), jnp.concatenate([qbias] * GQA, axis=1)], axis=0)
    s = jnp.dot(k_blk, rhs, preferred_element_type=F32)
    ps, sink_terms = [], []
    for g in range(GQA):
        sg = s[:, g * LANES:(g + 1) * LANES]
        sink = jnp.full((1, LANES), sinks[g], F32)
        m = jnp.maximum(jnp.max(sg, axis=0, keepdims=True), sink)
        ps.append(jnp.exp(sg - m).astype(BF))
        sink_terms.append(jnp.exp(sink - m))
    return jnp.concatenate(ps, axis=1), sink_terms


def _attn_mixer_pieces(a, t, *, ns, r, pos0, n_tiles, layer, make_kv):
    tm = ns * r
    nh = ns * WIN
    blocks = _row_blocks(ns, r)
    x_ref, mod_ref, kext_ref, vtext_ref = a["x"], a["mod"], a["kext"], a["vtext"]
    hb_ref, qt_ref, attnt_ref = a["hb"], a["qt"], a["attnt"]

    def v_rows(j):
        return slice(j * (HD + V_AUG_ROWS), j * (HD + V_AUG_ROWS) + HD)

    def setup():
        kprev = a["khist"][...].astype(BF)
        vtprev = a["vthist"][...].astype(BF)
        if n_tiles > 1:
            kprev = jnp.where(t == 0, kprev, kext_ref[:, r:r + WIN, 0:HD])
            vtprev = jnp.where(t == 0, vtprev, jnp.concatenate(
                [vtext_ref[v_rows(j), r:r + WIN] for j in range(NKV)], axis=0))
        feats = _key_mask_features(ns, r, pos0, t, nh + tm)
        ones_row = (lax.broadcasted_iota(jnp.int32, (V_AUG_ROWS, nh + tm), 0) == 0).astype(F32).astype(BF)
        for j in range(NKV):
            kext_ref[j] = feats
            vtext_ref[j * (HD + V_AUG_ROWS) + HD:(j + 1) * (HD + V_AUG_ROWS), :] = ones_row
            vtext_ref[v_rows(j), 0:nh] = vtprev[j * HD:(j + 1) * HD, :]
        kext_ref[:, 0:nh, 0:HD] = kprev
        if not make_kv:
            kext_ref[:, nh:, 0:HD] = a["knew"][...]
            for j in range(NKV):
                vtext_ref[v_rows(j), nh:] = a["vtnew"][j * HD:(j + 1) * HD, :]
        a["qbias"][...] = _query_mask_bias(ns, r, nh + tm)

    def rows_in(rows):
        x = _load_rows(x_ref, rows, ns, r)
        xn = x * lax.rsqrt(jnp.mean(x * x, axis=-1, keepdims=True) + EPS)
        hb_ref[rows, :] = (xn * (a["gmix"][0] * (1.0 + _mod(mod_ref, 1, ns, r))) + _mod(mod_ref, 0, ns, r)).astype(BF)
        if make_kv:
            modkv_ref = a["modkv"]
            a["hkv"][rows, :] = (
                xn * (a["gkv"][...] * (1.0 + _mod(modkv_ref, 1, ns, r))) + _mod(modkv_ref, 0, ns, r)).astype(BF)

    def rows_kv(rows):
        kv = jnp.dot(a["hkv"][rows, :], a["wkv_bf"][...], preferred_element_type=F32)
        v = kv[:, KVD:]
        ks = [_rms(kv[:, j * HD:(j + 1) * HD], a["gk"][...]) for j in range(NKV)]
        ext_rows = slice(nh + rows.start, nh + rows.stop)
        vt = v.T.astype(BF)
        for j in range(NKV):
            kj = ks[j].astype(BF)
            kext_ref[j, ext_rows, 0:HD] = kj
            a["kbf"][j, rows, :] = kj
            vtext_ref[v_rows(j), ext_rows] = vt[j * HD:(j + 1) * HD, :]
        a["vt"][:, rows] = vt
        tr = a["ktail"].shape[1]
        if ns > 1:
            for j in range(NKV):
                a["ktail"][:, :, j * HD:(j + 1) * HD] = ks[j].reshape(ns, r, HD)[:, r - tr:, :]
            a["vtail"][...] = v.reshape(ns, r, KVD)[:, r - tr:, :]
        elif rows.stop == r:
            n = rows.stop - rows.start
            for j in range(NKV):
                a["ktail"][0, :, j * HD:(j + 1) * HD] = ks[j][n - tr:, :]
            a["vtail"][0] = v[n - tr:, :]

    def q_heads(j):
        qt = lax.dot_general(a["wqt_bf"][j * GQA * HD:(j + 1) * GQA * HD, :], hb_ref[...], NT_DIMS,
                             preferred_element_type=F32)
        gq = jnp.concatenate([a["gq"][0] * (HD ** -0.5)] * (tm // LANES), axis=1)
        for g in range(GQA):
            blk = qt[g * HD:(g + 1) * HD, :]
            inv = lax.rsqrt(jnp.mean(blk * blk, axis=0, keepdims=True) + EPS)
            hd = j * GQA + g
            qt_ref[hd * HD:(hd + 1) * HD, :] = (blk * inv * gq).astype(BF)

    sink_terms = {}

    def scores(k, j, q0, k0, nk):
        heads = [j * GQA + g for g in range(GQA)]
        pt, sink_terms[k] = _softmax_t(
            kext_ref[j, k0:k0 + nk, :], [qt_ref[hd * HD:(hd + 1) * HD, q0:q0 + LANES] for hd in heads],
            a["qbias"][:, q0:q0 + LANES], [a["sinks"][layer, hd] for hd in heads])
        a["pt"][k % PT_SLOTS] = pt

    def values(k, j, q0, k0, nk):
        o = jnp.dot(vtext_ref[j * (HD + V_AUG_ROWS):(j + 1) * (HD + V_AUG_ROWS), k0:k0 + nk], a["pt"][k % PT_SLOTS],
                    preferred_element_type=F32)
        for g, sink_term in enumerate(sink_terms.pop(k)):
            hd = j * GQA + g
            lanes = slice(g * LANES, (g + 1) * LANES)
            inv = 1.0 / (o[HD:HD + 1, lanes] + sink_term)
            attnt_ref[hd * HD:(hd + 1) * HD, q0:q0 + LANES] = (o[0:HD, lanes] * inv).astype(BF)

    def rows_mix(rows):
        mix = lax.dot_general(attnt_ref[:, rows], a["wo_bf"][...], TN_DIMS, preferred_element_type=F32)
        x1 = _load_rows(x_ref, rows, ns, r) + _mod(mod_ref, 2, ns, r) * mix
        a["x1"][rows, :] = x1
        a["h2"][rows, :] = (_rms(x1, a["gffn"][0] * (1.0 + _mod(mod_ref, 4, ns, r))) + _mod(mod_ref, 3, ns, r)).astype(BF)

    if ns == 1:
        windows = [(q0, q0, WIN + LANES) for q0 in range(0, tm, LANES)]
    else:
        windows = [(0, 0, nh + tm)]
    pieces = [setup] + [functools.partial(rows_in, rows) for rows in blocks]
    if make_kv:
        pieces += [functools.partial(rows_kv, rows) for rows in blocks]
    pieces += [functools.partial(q_heads, j) for j in range(NKV)]
    att = [(k, j) + w for k, (j, w) in enumerate((j, w) for j in range(NKV) for w in windows)]
    pieces += _lagged([functools.partial(scores, *e) for e in att], [functools.partial(values, *e) for e in att],
                      PT_SLOTS - 1)
    return pieces + [functools.partial(rows_mix, rows) for rows in blocks]


def _layer_kernel(*refs, names, kind, streams, layer):
    refs = dict(zip(names, refs))
    n = pl.program_id(0)
    weights = [(refs["win"], refs["win_bf"]), (refs["wout"], refs["wout_bf"])]
    if kind != "pool":
        weights += [(refs["wq"], refs["wqt_bf"]), (refs["wo"], refs["wo_bf"])]
        if kind == "attn_kv":
            weights.append((refs["wkv"], refs["wkv_bf"]))
    _load_weight_chunks(n, WEIGHT_CHUNKS[kind], weights, transposed=[refs.get("wqt_bf")])
    if kind == "pool":
        @pl.when(n == 0)
        def _():
            refs["wpool_bf"][...] = refs["wpool"][0].astype(BF)

    for sfx, ns, r, pos0, n_tiles, first, n_steps in streams:
        a = dict(refs)
        a.update({k[:-len(sfx)]: v for k, v in refs.items() if sfx and k.endswith(sfx)})

        def tile(a=a, ns=ns, r=r, pos0=pos0, n_tiles=n_tiles, first=first):
            t = lax.rem(n - first, n_tiles)
            if kind == "pool":
                _run(_pool_mixer_pieces(a, t, ns=ns, r=r, pos0=pos0, n_tiles=n_tiles))
            else:
                _run(_attn_mixer_pieces(a, t, ns=ns, r=r, pos0=pos0, n_tiles=n_tiles, layer=layer,
                                        make_kv=kind == "attn_kv"))
            _run(_ffn_pieces(a["x1"], a["h2"], a["mod"], a["win_bf"], a["wout_bf"], a["act"], a["o"], ns=ns, r=r))

        pl.when((n >= first) & (n < first + n_steps))(tile)


def _ada_kernel(c_ref, w_ref, b_ref, o_ref):
    c = c_ref[...]
    c_act = (c * _sigmoid(c)).astype(BF)
    o_ref[0] = jnp.dot(c_act, w_ref[0].astype(BF), preferred_element_type=F32) + b_ref[0]


def _ada_call(c_all, w, b):
    nl, _, n = w.shape
    m = c_all.shape[0]
    nb = next(c for c in ADA_COL_BLOCKS if n % c == 0)
    return pl.pallas_call(
        _ada_kernel,
        grid=(nl, n // nb),
        in_specs=[
            pl.BlockSpec((m, D), lambda l, j: (0, 0)),
            pl.BlockSpec((1, D, nb), lambda l, j: (l, 0, j)),
            pl.BlockSpec((1, 1, nb), lambda l, j: (l, 0, j)),
        ],
        out_specs=pl.BlockSpec((1, m, nb), lambda l, j: (l, 0, j)),
        out_shape=jax.ShapeDtypeStruct((nl, m, n), F32),
        compiler_params=pltpu.CompilerParams(
            dimension_semantics=("arbitrary", "arbitrary"), vmem_limit_bytes=V7X_VMEM_LIMIT_BYTES),
        name="ada",
    )(c_all, w, b.reshape(nl, 1, n))


class _Specs:
    def __init__(self, ns, r, n_tiles, n_steps, first):
        self.ns, self.r, self.n_tiles = ns, r, n_tiles
        self.tile_of = lambda n: jnp.clip(n - first, 0, n_steps - 1)

    def tile(self, width):
        return pl.BlockSpec((self.ns, self.r, width),
                            lambda n: (self.tile_of(n) // self.n_tiles, self.tile_of(n) % self.n_tiles, 0))

    def group(self, rows, width):
        return pl.BlockSpec((self.ns, rows, width), lambda n: (self.tile_of(n) // self.n_tiles, 0, 0))

    def k_rows(self, rows_per_group=None):
        if rows_per_group is None:
            return pl.BlockSpec((NKV, self.ns * self.r, HD), lambda n: (0, self.tile_of(n), 0))
        return pl.BlockSpec((NKV, self.ns * rows_per_group, HD), lambda n: (0, self.tile_of(n) // self.n_tiles, 0))

    def vt_cols(self, rows_per_group=None):
        if rows_per_group is None:
            return pl.BlockSpec((KVD, self.ns * self.r), lambda n: (0, self.tile_of(n)))
        return pl.BlockSpec((KVD, self.ns * rows_per_group), lambda n: (0, self.tile_of(n) // self.n_tiles))

    @staticmethod
    def const(shape):
        return pl.BlockSpec(shape, lambda n: (0,) * len(shape), pipeline_mode=pl.Buffered(1))

    @staticmethod
    def layer(shape, i):
        return pl.BlockSpec((1,) + shape, lambda n: (i,) + (0,) * len(shape), pipeline_mode=pl.Buffered(1))

    @staticmethod
    def weight_chunk(n_chunks, rows, cols, i=None):
        assert rows % n_chunks == 0
        chunk = rows // n_chunks
        if i is None:
            return pl.BlockSpec((chunk, cols), lambda n: (jnp.minimum(n, n_chunks - 1), 0))
        return pl.BlockSpec((1, chunk, cols), lambda n: (i, jnp.minimum(n, n_chunks - 1), 0))


def _layer_call(kind, streams, wts, i, j=None):
    chunked = functools.partial(_Specs.weight_chunk, WEIGHT_CHUNKS[kind])
    ins = [
        ("gmix", wts["g_mix"], _Specs.layer((1, D), i)), ("gffn", wts["g_ffn"], _Specs.layer((1, D), i)),
        ("win", wts["w_ffn_in"], chunked(D, 2 * DFF, i)), ("wout", wts["w_ffn_out"], chunked(DFF, D, i)),
    ]
    outs = []
    scratch = [("win_bf", pltpu.VMEM((D, 2 * DFF), BF)), ("wout_bf", pltpu.VMEM((DFF, D), BF))]
    if kind == "pool":
        ins += [("wpool", wts["w_pool"], _Specs.layer((len(POOL_WINDOWS), PGD, PGD), i)),
                ("pscale", wts["pool_scale"], _Specs.layer((1, D), i))]
        scratch += [("wpool_bf", pltpu.VMEM((len(POOL_WINDOWS), PGD, PGD), BF))]
    else:
        ins += [("sinks", wts["sinks"], pl.BlockSpec(memory_space=pltpu.SMEM)),
                ("gq", wts["g_q"], _Specs.layer((HD, LANES), j)),
                ("wq", wts["w_q"], chunked(D, D, j)), ("wo", wts["w_o"], chunked(D, D, j))]
        scratch += [("wqt_bf", pltpu.VMEM((D, D), BF)), ("wo_bf", pltpu.VMEM((D, D), BF))]
        if kind == "attn_kv":
            ins += [("gkv", wts["g_kv"], _Specs.const((1, D))), ("wkv", wts["w_kv"], chunked(D, 2 * KVD)),
                    ("gk", wts["g_k"], _Specs.const((1, HD)))]
            scratch += [("wkv_bf", pltpu.VMEM((D, 2 * KVD), BF))]

    first = WEIGHT_CHUNKS[kind]
    stream_args, n_outs = [], []
    for k, st in enumerate(streams):
        sfx = "" if k == 0 else "_s%d" % k
        x, ns, r = st["x"], st["ns"], st["r"]
        bn, seq, _ = x.shape
        n_tiles = seq // r
        n_steps = (bn // ns) * n_tiles
        tm = ns * r
        nh = ns * WIN
        tr = min(r, WIN)
        assert ns == 1 or (n_tiles == 1 and r <= CHUNK)
        sp = _Specs(ns, r, n_tiles, n_steps, first)
        s_ins = [("x", x, sp.tile(D)), ("mod", st["mod"], sp.group(6, D))]
        s_outs = [("o", jax.ShapeDtypeStruct((bn, seq, D), F32), sp.tile(D))]
        s_scratch = [("x1", pltpu.VMEM((tm, D), F32)), ("h2", pltpu.VMEM((tm, D), BF)),
                     ("act", pltpu.VMEM((tm, DFF), BF))]
        if kind == "pool":
            s_ins += [("hist", st["pool_hist"], sp.group(HALO, D))]
            s_outs += [("tail", jax.ShapeDtypeStruct((bn, HALO, D), F32), sp.group(HALO, D))]
            s_scratch += [("halo", pltpu.VMEM((ns, HALO, D), F32)), ("diff", pltpu.VMEM((tm, D), BF))]
        else:
            s_ins += [("khist", st["khist"], sp.k_rows(WIN)), ("vthist", st["vthist"], sp.vt_cols(WIN))]
            s_scratch += [
                ("kext", pltpu.VMEM((NKV, nh + tm, 2 * HD), BF)),
                ("vtext", pltpu.VMEM((NKV * (HD + V_AUG_ROWS), nh + tm), BF)), ("qbias", pltpu.VMEM((HD, tm), BF)),
                ("hb", pltpu.VMEM((tm, D), BF)), ("qt", pltpu.VMEM((D, tm), BF)), ("attnt", pltpu.VMEM((D, tm), BF)),
                ("pt", pltpu.VMEM((PT_SLOTS, WIN + LANES if ns == 1 else nh + tm, GQA * LANES), BF)),
            ]
            if kind == "attn_kv":
                s_ins += [("modkv", st["modkv"], sp.group(2, D))]
                s_outs += [
                    ("ktail", jax.ShapeDtypeStruct((bn, tr, KVD), F32), sp.group(tr, KVD)),
                    ("vtail", jax.ShapeDtypeStruct((bn, tr, KVD), F32), sp.group(tr, KVD)),
                    ("kbf", jax.ShapeDtypeStruct((NKV, bn * seq, HD), BF), sp.k_rows()),
                    ("vt", jax.ShapeDtypeStruct((KVD, bn * seq), BF), sp.vt_cols()),
                ]
                s_scratch += [("hkv", pltpu.VMEM((tm, D), BF))]
            else:
                s_ins += [("knew", st["kbf"], sp.k_rows()), ("vtnew", st["vt"], sp.vt_cols())]
        ins += [(e[0] + sfx,) + e[1:] for e in s_ins]
        outs += [(e[0] + sfx,) + e[1:] for e in s_outs]
        scratch += [(e[0] + sfx,) + e[1:] for e in s_scratch]
        stream_args.append((sfx, ns, r, st["pos0"], n_tiles, first, n_steps))
        n_outs.append(len(s_outs))
        first += n_steps

    names = [e[0] for e in ins] + [e[0] for e in outs] + [e[0] for e in scratch]
    flat = pl.pallas_call(
        functools.partial(_layer_kernel, names=names, kind=kind, streams=tuple(stream_args), layer=j),
        grid=(first,),
        in_specs=[e[2] for e in ins],
        out_specs=[e[2] for e in outs],
        out_shape=[e[1] for e in outs],
        scratch_shapes=[e[1] for e in scratch],
        compiler_params=pltpu.CompilerParams(
            dimension_semantics=("arbitrary",), vmem_limit_bytes=V7X_VMEM_LIMIT_BYTES),
        name=kind + "_layer",
    )(*[e[1] for e in ins])
    results, at = [], 0
    for count in n_outs:
        results.append(list(flat[at:at + count]))
        at += count
    return results


def _trunks(streams, wts, n_a, n_b):
    xs = [st["x"] for st in streams]
    tails = [[] for _ in streams]

    def layer_streams(i, tile_rows, **extra):
        return [dict(x=xs[k], mod=st["mod"][i], ns=st["ns"], r=min(st["seq_rows"], tile_rows), pos0=st["pos0"],
                     **{name: vals[k] for name, vals in extra.items()}) for k, st in enumerate(streams)]

    for i in range(n_a):
        res = _layer_call("pool", layer_streams(i, TILE_ROWS_POOL, pool_hist=[st["pool_hist"][i] for st in streams]),
                          wts, i)
        for k, (x, tail) in enumerate(res):
            xs[k] = x
            tails[k].append(tail[:, 1:, :])
    khist = [st["k_hist"].transpose(2, 0, 1, 3).reshape(NKV, -1, HD) for st in streams]
    vthist = [st["v_hist"].transpose(2, 3, 0, 1).reshape(KVD, -1) for st in streams]
    res = _layer_call("attn_kv", layer_streams(n_a, TILE_ROWS_ATTN_KV, khist=khist, vthist=vthist,
                                               modkv=[st["modkv"] for st in streams]), wts, n_a, 0)
    xs = [r[0] for r in res]
    kv = [r[1:] for r in res]
    for j in range(1, n_b):
        res = _layer_call("attn", layer_streams(n_a + j, TILE_ROWS_ATTN, khist=khist, vthist=vthist,
                                                kbf=[e[2] for e in kv], vt=[e[3] for e in kv]), wts, n_a + j, j)
        xs = [r[0] for r in res]
    out = []
    for k, st in enumerate(streams):
        bn = xs[k].shape[0]
        k_tail = jnp.concatenate([st["k_hist"].reshape(bn, WIN, KVD), kv[k][0]], axis=1)[:, -WIN:]
        v_tail = jnp.concatenate([st["v_hist"].reshape(bn, WIN, KVD), kv[k][1]], axis=1)[:, -WIN:]
        out.append((xs[k], jnp.stack(tails[k], axis=0), k_tail.reshape(bn, WIN, NKV, HD),
                    v_tail.reshape(bn, WIN, NKV, HD)))
    return out


def kernel(x_prompt, x_sample, c_prompt, c_sample, state_pool, cache_k, cache_v, w_ada, b_ada, g_mix, g_ffn, w_pool, pool_scale, w_q, g_q, sinks, w_o, g_kv, w_ada_kv, b_ada_kv, w_kv, g_k, w_ffn_in, w_ffn_out):
    depth = w_ada.shape[0]
    n_a = state_pool.shape[0]
    bp = x_prompt.shape[0]
    bs, ls, _ = x_sample.shape

    c_all = jnp.concatenate([c_prompt, c_sample, jnp.zeros((16 - bp - bs, D), F32)], axis=0)
    ada = _ada_call(c_all, w_ada, b_ada).reshape(depth, 16, 6, D)
    ada_kv = _ada_call(c_all, w_ada_kv[None], b_ada_kv[None]).reshape(16, 2, D)

    wts = dict(
        g_mix=g_mix.reshape(depth, 1, D), g_ffn=g_ffn.reshape(depth, 1, D),
        w_pool=w_pool, pool_scale=pool_scale.reshape(n_a, 1, D),
        w_q=w_q, g_q=jnp.broadcast_to(g_q[:, :, None], g_q.shape + (LANES,)), sinks=sinks, w_o=w_o,
        g_kv=g_kv.reshape(1, D), w_kv=w_kv, g_k=g_k.reshape(1, HD),
        w_ffn_in=w_ffn_in, w_ffn_out=w_ffn_out,
    )

    zero_kv = jnp.zeros((bp, WIN, NKV, HD), F32)
    prompt = dict(x=x_prompt, mod=ada[:, :bp], modkv=ada_kv[:bp], pool_hist=jnp.zeros((n_a, bp, HALO, D), F32),
                  k_hist=zero_kv, v_hist=zero_kv, ns=1, seq_rows=x_prompt.shape[1], pos0=0)
    sample = dict(x=x_sample, mod=ada[:, bp:bp + bs], modkv=ada_kv[bp:bp + bs],
                  pool_hist=jnp.pad(state_pool, ((0, 0), (0, 0), (HALO - state_pool.shape[2], 0), (0, 0))),
                  k_hist=cache_k, v_hist=cache_v, ns=bs, seq_rows=ls, pos0=PAST_LEN)
    (y_p, pool_p, k_p, v_p), (y_s, pool_s, k_s, v_s) = _trunks([prompt, sample], wts, n_a, depth - n_a)
    return (y_p, y_s, pool_p, k_p, v_p, pool_s, k_s, v_s)
```

```python
import functools

import jax
import jax.numpy as jnp
from jax import lax
from jax.experimental import pallas as pl
from jax.experimental.pallas import tpu as pltpu

D = 1024
DFF = 2816
HD = 64
NH = 16
NKV = 4
GQA = NH // NKV
KVD = NKV * HD
WIN = 128
CHUNK = 64
PAST_LEN = 4096
POOL_WINDOWS = (2, 4, 8, 16)
PGD = D // len(POOL_WINDOWS)
V_AUG_ROWS = 16
HALO = 16
LANES = 128
EPS = 1e-6
NEG_INF = -1e30
BF = jnp.bfloat16
F32 = jnp.float32

V7X_VMEM_LIMIT_BYTES = 60 * 1024 * 1024
TILE_ROWS_POOL = 512
TILE_ROWS_ATTN_KV = 256
TILE_ROWS_ATTN = 512
FFN_COL_CHUNK = 256
ROW_PIECE = 512
ATTN_ROW_PIECE = 512
FFN_OUT_ROWS = 512
PT_SLOTS = 3
WEIGHT_CHUNKS = {"pool": 8, "attn_kv": 8, "attn": 16}
ADA_COL_BLOCKS = (1536, 1024, 128)

NT_DIMS = (((1,), (1,)), ((), ()))
TN_DIMS = (((0,), (0,)), ((), ()))


def _rms(x, g):
    return x * lax.rsqrt(jnp.mean(x * x, axis=-1, keepdims=True) + EPS) * g


def _sigmoid(x):
    return 1.0 / (1.0 + jnp.exp(-x))


def _mod(mod_ref, k, ns, r):
    if ns == 1:
        return mod_ref[0, k:k + 1, :]
    return jnp.concatenate([jnp.broadcast_to(mod_ref[s, k:k + 1, :], (r, D)) for s in range(ns)], axis=0)


def _row_blocks(ns, r):
    tm = ns * r
    rb = min(tm, ATTN_ROW_PIECE)
    assert ns == 1 or rb == tm
    return [slice(b * rb, (b + 1) * rb) for b in range(tm // rb)]


def _load_rows(ref, rows, ns, r):
    return ref[0, rows, :] if ns == 1 else ref[...].reshape(ns * r, ref.shape[-1])


def _store_rows(ref, rows, val, ns, r):
    if ns == 1:
        ref[0, rows, :] = val
    else:
        ref[...] = val.reshape(ns, r, ref.shape[-1])


def _run(pieces):
    for piece in pieces:
        piece()


def _lagged(firsts, seconds, lag):
    out = []
    for k in range(len(firsts) + lag):
        if k < len(firsts):
            out.append(firsts[k])
        if k >= lag:
            out.append(seconds[k - lag])
    return out


def _load_weight_chunks(n, n_chunks, pairs, transposed=()):
    def load(c):
        for src, dst in pairs:
            chunk = src[...].reshape(src.shape[-2:])
            rows = chunk.shape[0]
            if any(dst is ref for ref in transposed):
                dst[:, c * rows:(c + 1) * rows] = chunk.T.astype(BF)
            else:
                dst[c * rows:(c + 1) * rows, :] = chunk.astype(BF)

    for c in range(n_chunks):
        pl.when(n == c)(functools.partial(load, c))


def _ffn_pieces(x1_ref, h2_ref, mod_ref, win_ref, wout_ref, act_ref, o_ref, *, ns, r):
    tm = ns * r
    rb = min(tm, FFN_OUT_ROWS)
    blocks = [slice(b * rb, (b + 1) * rb) for b in range(tm // rb)]

    def chunk(c):
        lo = c * FFN_COL_CHUNK
        hi = lo + FFN_COL_CHUNK
        gate = jnp.dot(h2_ref[...], win_ref[:, lo:hi], preferred_element_type=F32)
        up = jnp.dot(h2_ref[...], win_ref[:, DFF + lo:DFF + hi], preferred_element_type=F32)
        act_ref[:, lo:hi] = (gate * _sigmoid(gate) * up).astype(BF)

    def rows_out(rows):
        y = jnp.dot(act_ref[rows, :], wout_ref[...], preferred_element_type=F32)
        _store_rows(o_ref, rows, x1_ref[rows, :] + _mod(mod_ref, 5, ns, r) * y, ns, r)

    return ([functools.partial(chunk, c) for c in range(DFF // FFN_COL_CHUNK)]
            + [functools.partial(rows_out, rows) for rows in blocks])


def _pool_mixer_pieces(a, t, *, ns, r, pos0, n_tiles):
    rp = min(r, ROW_PIECE)
    x_ref, mod_ref, halo_ref, diff_ref = a["x"], a["mod"], a["halo"], a["diff"]

    def pool(s, p):
        lo = p * rp
        rows = slice(s * r + lo, s * r + lo + rp)

        def m(k):
            return mod_ref[s, k:k + 1, :]

        if p > 0:
            halo = halo_ref[s]
        elif n_tiles == 1:
            halo = a["hist"][s]
        else:
            halo = jnp.where(t == 0, a["hist"][s], halo_ref[s])
        x = x_ref[s, lo:lo + rp, :]
        h = _rms(x, a["gmix"][0] * (1.0 + m(1))) + m(0)
        halo_ref[s] = h[rp - HALO:, :]
        if lo + rp == r:
            a["tail"][s] = h[rp - HALO:, :]

        pos = pos0 + t * r + lo + lax.broadcasted_iota(jnp.int32, (rp, 1), 0)
        for g, w in enumerate(POOL_WINDOWS):
            cols = slice(g * PGD, (g + 1) * PGD)
            acc = jnp.concatenate([halo[:, cols], h[:, cols]], axis=0)
            step = 1
            while step < w:
                acc = acc + pltpu.roll(acc, step, 0)
                step *= 2
            inv_count = 1.0 / jnp.minimum(w, pos + 1).astype(F32)
            diff_ref[rows, cols] = (acc[HALO:, :] * inv_count - h[:, cols]).astype(BF)

    def mix(s, p):
        lo = p * rp
        rows = slice(s * r + lo, s * r + lo + rp)

        def m(k):
            return mod_ref[s, k:k + 1, :]

        mixed = jnp.concatenate(
            [jnp.dot(diff_ref[rows, g * PGD:(g + 1) * PGD], a["wpool_bf"][g], preferred_element_type=F32)
             for g in range(len(POOL_WINDOWS))], axis=1)
        x1 = x_ref[s, lo:lo + rp, :] + (m(2) * a["pscale"][0]) * mixed
        a["x1"][rows, :] = x1
        a["h2"][rows, :] = (_rms(x1, a["gffn"][0] * (1.0 + m(4))) + m(3)).astype(BF)

    sp = [(s, p) for s in range(ns) for p in range(r // rp)]
    return _lagged([functools.partial(pool, *e) for e in sp], [functools.partial(mix, *e) for e in sp], 1)


def _key_mask_features(ns, r, pos0, t, nrows):
    e = lax.broadcasted_iota(jnp.int32, (nrows, LANES), 0)
    f = lax.broadcasted_iota(jnp.int32, (nrows, LANES), 1) - HD
    if ns == 1:
        group = e // CHUNK
        before_start = e < jnp.maximum(0, WIN - (pos0 + t * r))
    else:
        nh = ns * WIN
        is_hist = e < nh
        group = jnp.where(is_hist, e // WIN, (e - nh) // r)
        before_start = is_hist & (e % WIN < WIN - pos0)
    n_groups = _n_key_groups(ns, nrows)
    return ((f == group) | ((f == n_groups) & before_start)).astype(F32).astype(BF)


def _query_mask_bias(ns, r, nrows):
    tm = ns * r
    c = lax.broadcasted_iota(jnp.int32, (HD, tm), 0)
    q = lax.broadcasted_iota(jnp.int32, (HD, tm), 1)
    n_groups = _n_key_groups(ns, nrows)
    if ns == 1:
        d = c - q // CHUNK
        hidden = (d < 0) | (d > WIN // CHUNK)
    else:
        hidden = c != q // r
    hidden = ((c < n_groups) & hidden) | (c == n_groups)
    return jnp.where(hidden, NEG_INF, 0.0).astype(BF)


def _n_key_groups(ns, nrows):
    n_groups = nrows // CHUNK if ns == 1 else ns
    assert n_groups + 1 <= HD
    return n_groups


def _softmax_t(k_blk, qt_blks, qbias, sinks):
    rhs = jnp.concatenate([jnp.concatenate(qt_blks, axis=1), jnp.concatenate([qbias] * GQA, axis=1)], axis=0)
    s = jnp.dot(k_blk, rhs, preferred_element_type=F32)
    ps, sink_terms = [], []
    for g in range(GQA):
        sg = s[:, g * LANES:(g + 1) * LANES]
        sink = jnp.full((1, LANES), sinks[g], F32)
        m = jnp.maximum(jnp.max(sg, axis=0, keepdims=True), sink)
        ps.append(jnp.exp(sg - m).astype(BF))
        sink_terms.append(jnp.exp(sink - m))
    return jnp.concatenate(ps, axis=1), sink_terms


def _attn_mixer_pieces(a, t, *, ns, r, pos0, n_tiles, layer, make_kv):
    tm = ns * r
    nh = ns * WIN
    blocks = _row_blocks(ns, r)
    x_ref, mod_ref, kext_ref, vtext_ref = a["x"], a["mod"], a["kext"], a["vtext"]
    hb_ref, qt_ref, attnt_ref = a["hb"], a["qt"], a["attnt"]

    def v_rows(j):
        return slice(j * (HD + V_AUG_ROWS), j * (HD + V_AUG_ROWS) + HD)

    def setup():
        kprev = a["khist"][...].astype(BF)
        vtprev = a["vthist"][...].astype(BF)
        if n_tiles > 1:
            kprev = jnp.where(t == 0, kprev, kext_ref[:, r:r + WIN, 0:HD])
            vtprev = jnp.where(t == 0, vtprev, jnp.concatenate(
                [vtext_ref[v_rows(j), r:r + WIN] for j in range(NKV)], axis=0))
        feats = _key_mask_features(ns, r, pos0, t, nh + tm)
        ones_row = (lax.broadcasted_iota(jnp.int32, (V_AUG_ROWS, nh + tm), 0) == 0).astype(F32).astype(BF)
        for j in range(NKV):
            kext_ref[j] = feats
            vtext_ref[j * (HD + V_AUG_ROWS) + HD:(j + 1) * (HD + V_AUG_ROWS), :] = ones_row
            vtext_ref[v_rows(j), 0:nh] = vtprev[j * HD:(j + 1) * HD, :]
        a["qbias"][...] = _query_mask_bias(ns, r, nh + tm)
        kext_ref[:, 0:nh, 0:HD] = kprev
        if not make_kv:
            kext_ref[:, nh:, 0:HD] = a["knew"][...]
            for j in range(NKV):
                vtext_ref[v_rows(j), nh:] = a["vtnew"][j * HD:(j + 1) * HD, :]

    def rows_in(rows):
        x = _load_rows(x_ref, rows, ns, r)
        xn = x * lax.rsqrt(jnp.mean(x * x, axis=-1, keepdims=True) + EPS)
        hb_ref[rows, :] = (xn * (a["gmix"][0] * (1.0 + _mod(mod_ref, 1, ns, r))) + _mod(mod_ref, 0, ns, r)).astype(BF)
        if make_kv:
            modkv_ref = a["modkv"]
            a["hkv"][rows, :] = (
                xn * (a["gkv"][...] * (1.0 + _mod(modkv_ref, 1, ns, r))) + _mod(modkv_ref, 0, ns, r)).astype(BF)

    def rows_kv(rows):
        kv = jnp.dot(a["hkv"][rows, :], a["wkv_bf"][...], preferred_element_type=F32)
        v = kv[:, KVD:]
        ks = [_rms(kv[:, j * HD:(j + 1) * HD], a["gk"][...]) for j in range(NKV)]
        ext_rows = slice(nh + rows.start, nh + rows.stop)
        vt = v.T.astype(BF)
        for j in range(NKV):
            kj = ks[j].astype(BF)
            kext_ref[j, ext_rows, 0:HD] = kj
            a["kbf"][j, rows, :] = kj
            vtext_ref[v_rows(j), ext_rows] = vt[j * HD:(j + 1) * HD, :]
        a["vt"][:, rows] = vt
        tr = a["ktail"].shape[1]
        if ns > 1:
            for j in range(NKV):
                a["ktail"][:, :, j * HD:(j + 1) * HD] = ks[j].reshape(ns, r, HD)[:, r - tr:, :]
            a["vtail"][...] = v.reshape(ns, r, KVD)[:, r - tr:, :]
        elif rows.stop == r:
            n = rows.stop - rows.start
            for j in range(NKV):
                a["ktail"][0, :, j * HD:(j + 1) * HD] = ks[j][n - tr:, :]
            a["vtail"][0] = v[n - tr:, :]

    def q_heads(j):
        qt = lax.dot_general(a["wqt_bf"][j * GQA * HD:(j + 1) * GQA * HD, :], hb_ref[...], NT_DIMS,
                             preferred_element_type=F32)
        gq = jnp.concatenate([a["gq"][0] * (HD ** -0.5)] * (tm // LANES), axis=1)
        for g in range(GQA):
            blk = qt[g * HD:(g + 1) * HD, :]
            inv = lax.rsqrt(jnp.mean(blk * blk, axis=0, keepdims=True) + EPS)
            hd = j * GQA + g
            qt_ref[hd * HD:(hd + 1) * HD, :] = (blk * inv * gq).astype(BF)

    sink_terms = {}

    def scores(k, j, q0, k0, nk):
        heads = [j * GQA + g for g in range(GQA)]
        pt, sink_terms[k] = _softmax_t(
            kext_ref[j, k0:k0 + nk, :], [qt_ref[hd * HD:(hd + 1) * HD, q0:q0 + LANES] for hd in heads],
            a["qbias"][:, q0:q0 + LANES], [a["sinks"][layer, hd] for hd in heads])
        a["pt"][k % PT_SLOTS] = pt

    def values(k, j, q0, k0, nk):
        o = jnp.dot(vtext_ref[j * (HD + V_AUG_ROWS):(j + 1) * (HD + V_AUG_ROWS), k0:k0 + nk], a["pt"][k % PT_SLOTS],
                    preferred_element_type=F32)
        for g, sink_term in enumerate(sink_terms.pop(k)):
            hd = j * GQA + g
            lanes = slice(g * LANES, (g + 1) * LANES)
            inv = 1.0 / (o[HD:HD + 1, lanes] + sink_term)
            attnt_ref[hd * HD:(hd + 1) * HD, q0:q0 + LANES] = (o[0:HD, lanes] * inv).astype(BF)

    def rows_mix(rows):
        mix = lax.dot_general(attnt_ref[:, rows], a["wo_bf"][...], TN_DIMS, preferred_element_type=F32)
        x1 = _load_rows(x_ref, rows, ns, r) + _mod(mod_ref, 2, ns, r) * mix
        a["x1"][rows, :] = x1
        a["h2"][rows, :] = (_rms(x1, a["gffn"][0] * (1.0 + _mod(mod_ref, 4, ns, r))) + _mod(mod_ref, 3, ns, r)).astype(BF)

    if ns == 1:
        windows = [(q0, q0, WIN + LANES) for q0 in range(0, tm, LANES)]
    else:
        windows = [(0, 0, nh + tm)]
    pieces = [setup] + [functools.partial(rows_in, rows) for rows in blocks]
    if make_kv:
        pieces += [functools.partial(rows_kv, rows) for rows in blocks]
    pieces += [functools.partial(q_heads, j) for j in range(NKV)]
    att = [(k, j) + w for k, (j, w) in enumerate((j, w) for j in range(NKV) for w in windows)]
    pieces += _lagged([functools.partial(scores, *e) for e in att], [functools.partial(values, *e) for e in att],
                      PT_SLOTS - 1)
    return pieces + [functools.partial(rows_mix, rows) for rows in blocks]


def _layer_kernel(*refs, names, kind, streams, layer):
    refs = dict(zip(names, refs))
    n = pl.program_id(0)
    weights = [(refs["win"], refs["win_bf"]), (refs["wout"], refs["wout_bf"])]
    if kind != "pool":
        weights += [(refs["wq"], refs["wqt_bf"]), (refs["wo"], refs["wo_bf"])]
        if kind == "attn_kv":
            weights.append((refs["wkv"], refs["wkv_bf"]))
    _load_weight_chunks(n, WEIGHT_CHUNKS[kind], weights, transposed=[refs.get("wqt_bf")])
    if kind == "pool":
        @pl.when(n == 0)
        def _():
            refs["wpool_bf"][...] = refs["wpool"][0].astype(BF)

    for sfx, ns, r, pos0, n_tiles, first, n_steps in streams:
        a = dict(refs)
        a.update({k[:-len(sfx)]: v for k, v in refs.items() if sfx and k.endswith(sfx)})

        def tile(a=a, ns=ns, r=r, pos0=pos0, n_tiles=n_tiles, first=first):
            t = lax.rem(n - first, n_tiles)
            if kind == "pool":
                _run(_pool_mixer_pieces(a, t, ns=ns, r=r, pos0=pos0, n_tiles=n_tiles))
            else:
                _run(_attn_mixer_pieces(a, t, ns=ns, r=r, pos0=pos0, n_tiles=n_tiles, layer=layer,
                                        make_kv=kind == "attn_kv"))
            _run(_ffn_pieces(a["x1"], a["h2"], a["mod"], a["win_bf"], a["wout_bf"], a["act"], a["o"], ns=ns, r=r))

        pl.when((n >= first) & (n < first + n_steps))(tile)


def _ada_kernel(c_ref, w_ref, b_ref, o_ref):
    c = c_ref[...]
    c_act = (c * _sigmoid(c)).astype(BF)
    o_ref[0] = jnp.dot(c_act, w_ref[0].astype(BF), preferred_element_type=F32) + b_ref[0]


def _ada_call(c_all, w, b):
    nl, _, n = w.shape
    m = c_all.shape[0]
    nb = next(c for c in ADA_COL_BLOCKS if n % c == 0)
    return pl.pallas_call(
        _ada_kernel,
        grid=(nl, n // nb),
        in_specs=[
            pl.BlockSpec((m, D), lambda l, j: (0, 0)),
            pl.BlockSpec((1, D, nb), lambda l, j: (l, 0, j)),
            pl.BlockSpec((1, 1, nb), lambda l, j: (l, 0, j)),
        ],
        out_specs=pl.BlockSpec((1, m, nb), lambda l, j: (l, 0, j)),
        out_shape=jax.ShapeDtypeStruct((nl, m, n), F32),
        compiler_params=pltpu.CompilerParams(
            dimension_semantics=("arbitrary", "arbitrary"), vmem_limit_bytes=V7X_VMEM_LIMIT_BYTES),
        name="ada",
    )(c_all, w, b.reshape(nl, 1, n))


class _Specs:
    def __init__(self, ns, r, n_tiles, n_steps, first):
        self.ns, self.r, self.n_tiles = ns, r, n_tiles
        self.tile_of = lambda n: jnp.clip(n - first, 0, n_steps - 1)

    def tile(self, width):
        return pl.BlockSpec((self.ns, self.r, width),
                            lambda n: (self.tile_of(n) // self.n_tiles, self.tile_of(n) % self.n_tiles, 0))

    def group(self, rows, width):
        return pl.BlockSpec((self.ns, rows, width), lambda n: (self.tile_of(n) // self.n_tiles, 0, 0))

    def k_rows(self, rows_per_group=None):
        if rows_per_group is None:
            return pl.BlockSpec((NKV, self.ns * self.r, HD), lambda n: (0, self.tile_of(n), 0))
        return pl.BlockSpec((NKV, self.ns * rows_per_group, HD), lambda n: (0, self.tile_of(n) // self.n_tiles, 0))

    def vt_cols(self, rows_per_group=None):
        if rows_per_group is None:
            return pl.BlockSpec((KVD, self.ns * self.r), lambda n: (0, self.tile_of(n)))
        return pl.BlockSpec((KVD, self.ns * rows_per_group), lambda n: (0, self.tile_of(n) // self.n_tiles))

    @staticmethod
    def const(shape):
        return pl.BlockSpec(shape, lambda n: (0,) * len(shape), pipeline_mode=pl.Buffered(1))

    @staticmethod
    def layer(shape, i):
        return pl.BlockSpec((1,) + shape, lambda n: (i,) + (0,) * len(shape), pipeline_mode=pl.Buffered(1))

    @staticmethod
    def weight_chunk(n_chunks, rows, cols, i=None):
        assert rows % n_chunks == 0
        chunk = rows // n_chunks
        if i is None:
            return pl.BlockSpec((chunk, cols), lambda n: (jnp.minimum(n, n_chunks - 1), 0))
        return pl.BlockSpec((1, chunk, cols), lambda n: (i, jnp.minimum(n, n_chunks - 1), 0))


def _layer_call(kind, streams, wts, i, j=None):
    chunked = functools.partial(_Specs.weight_chunk, WEIGHT_CHUNKS[kind])
    ins = [
        ("gmix", wts["g_mix"], _Specs.layer((1, D), i)), ("gffn", wts["g_ffn"], _Specs.layer((1, D), i)),
        ("win", wts["w_ffn_in"], chunked(D, 2 * DFF, i)), ("wout", wts["w_ffn_out"], chunked(DFF, D, i)),
    ]
    outs = []
    scratch = [("win_bf", pltpu.VMEM((D, 2 * DFF), BF)), ("wout_bf", pltpu.VMEM((DFF, D), BF))]
    if kind == "pool":
        ins += [("wpool", wts["w_pool"], _Specs.layer((len(POOL_WINDOWS), PGD, PGD), i)),
                ("pscale", wts["pool_scale"], _Specs.layer((1, D), i))]
        scratch += [("wpool_bf", pltpu.VMEM((len(POOL_WINDOWS), PGD, PGD), BF))]
    else:
        ins += [("sinks", wts["sinks"], pl.BlockSpec(memory_space=pltpu.SMEM)),
                ("gq", wts["g_q"], _Specs.layer((HD, LANES), j)),
                ("wq", wts["w_q"], chunked(D, D, j)), ("wo", wts["w_o"], chunked(D, D, j))]
        scratch += [("wqt_bf", pltpu.VMEM((D, D), BF)), ("wo_bf", pltpu.VMEM((D, D), BF))]
        if kind == "attn_kv":
            ins += [("gkv", wts["g_kv"], _Specs.const((1, D))), ("wkv", wts["w_kv"], chunked(D, 2 * KVD)),
                    ("gk", wts["g_k"], _Specs.const((1, HD)))]
            scratch += [("wkv_bf", pltpu.VMEM((D, 2 * KVD), BF))]

    first = WEIGHT_CHUNKS[kind]
    stream_args, n_outs = [], []
    for k, st in enumerate(streams):
        sfx = "" if k == 0 else "_s%d" % k
        x, ns, r = st["x"], st["ns"], st["r"]
        bn, seq, _ = x.shape
        n_tiles = seq // r
        n_steps = (bn // ns) * n_tiles
        tm = ns * r
        nh = ns * WIN
        tr = min(r, WIN)
        assert ns == 1 or (n_tiles == 1 and r <= CHUNK)
        sp = _Specs(ns, r, n_tiles, n_steps, first)
        s_ins = [("x", x, sp.tile(D)), ("mod", st["mod"], sp.group(6, D))]
        s_outs = [("o", jax.ShapeDtypeStruct((bn, seq, D), F32), sp.tile(D))]
        s_scratch = [("x1", pltpu.VMEM((tm, D), F32)), ("h2", pltpu.VMEM((tm, D), BF)),
                     ("act", pltpu.VMEM((tm, DFF), BF))]
        if kind == "pool":
            s_ins += [("hist", st["pool_hist"], sp.group(HALO, D))]
            s_outs += [("tail", jax.ShapeDtypeStruct((bn, HALO, D), F32), sp.group(HALO, D))]
            s_scratch += [("halo", pltpu.VMEM((ns, HALO, D), F32)), ("diff", pltpu.VMEM((tm, D), BF))]
        else:
            s_ins += [("khist", st["khist"], sp.k_rows(WIN)), ("vthist", st["vthist"], sp.vt_cols(WIN))]
            s_scratch += [
                ("kext", pltpu.VMEM((NKV, nh + tm, 2 * HD), BF)),
                ("vtext", pltpu.VMEM((NKV * (HD + V_AUG_ROWS), nh + tm), BF)), ("qbias", pltpu.VMEM((HD, tm), BF)),
                ("hb", pltpu.VMEM((tm, D), BF)), ("qt", pltpu.VMEM((D, tm), BF)), ("attnt", pltpu.VMEM((D, tm), BF)),
                ("pt", pltpu.VMEM((PT_SLOTS, WIN + LANES if ns == 1 else nh + tm, GQA * LANES), BF)),
            ]
            if kind == "attn_kv":
                s_ins += [("modkv", st["modkv"], sp.group(2, D))]
                s_outs += [
                    ("ktail", jax.ShapeDtypeStruct((bn, tr, KVD), F32), sp.group(tr, KVD)),
                    ("vtail", jax.ShapeDtypeStruct((bn, tr, KVD), F32), sp.group(tr, KVD)),
                    ("kbf", jax.ShapeDtypeStruct((NKV, bn * seq, HD), BF), sp.k_rows()),
                    ("vt", jax.ShapeDtypeStruct((KVD, bn * seq), BF), sp.vt_cols()),
                ]
                s_scratch += [("hkv", pltpu.VMEM((tm, D), BF))]
            else:
                s_ins += [("knew", st["kbf"], sp.k_rows()), ("vtnew", st["vt"], sp.vt_cols())]
        ins += [(e[0] + sfx,) + e[1:] for e in s_ins]
        outs += [(e[0] + sfx,) + e[1:] for e in s_outs]
        scratch += [(e[0] + sfx,) + e[1:] for e in s_scratch]
        stream_args.append((sfx, ns, r, st["pos0"], n_tiles, first, n_steps))
        n_outs.append(len(s_outs))
        first += n_steps

    names = [e[0] for e in ins] + [e[0] for e in outs] + [e[0] for e in scratch]
    flat = pl.pallas_call(
        functools.partial(_layer_kernel, names=names, kind=kind, streams=tuple(stream_args), layer=j),
        grid=(first,),
        in_specs=[e[2] for e in ins],
        out_specs=[e[2] for e in outs],
        out_shape=[e[1] for e in outs],
        scratch_shapes=[e[1] for e in scratch],
        compiler_params=pltpu.CompilerParams(
            dimension_semantics=("arbitrary",), vmem_limit_bytes=V7X_VMEM_LIMIT_BYTES),
        name=kind + "_layer",
    )(*[e[1] for e in ins])
    results, at = [], 0
    for count in n_outs:
        results.append(list(flat[at:at + count]))
        at += count
    return results


def _trunks(streams, wts, n_a, n_b):
    xs = [st["x"] for st in streams]
    tails = [[] for _ in streams]

    def layer_streams(i, tile_rows, **extra):
        return [dict(x=xs[k], mod=st["mod"][i], ns=st["ns"], r=min(st["seq_rows"], tile_rows), pos0=st["pos0"],
                     **{name: vals[k] for name, vals in extra.items()}) for k, st in enumerate(streams)]

    for i in range(n_a):
        res = _layer_call("pool", layer_streams(i, TILE_ROWS_POOL, pool_hist=[st["pool_hist"][i] for st in streams]),
                          wts, i)
        for k, (x, tail) in enumerate(res):
            xs[k] = x
            tails[k].append(tail[:, 1:, :])
    khist = [st["k_hist"].transpose(2, 0, 1, 3).reshape(NKV, -1, HD) for st in streams]
    vthist = [st["v_hist"].transpose(2, 3, 0, 1).reshape(KVD, -1) for st in streams]
    res = _layer_call("attn_kv", layer_streams(n_a, TILE_ROWS_ATTN_KV, khist=khist, vthist=vthist,
                                               modkv=[st["modkv"] for st in streams]), wts, n_a, 0)
    xs = [r[0] for r in res]
    kv = [r[1:] for r in res]
    for j in range(1, n_b):
        res = _layer_call("attn", layer_streams(n_a + j, TILE_ROWS_ATTN, khist=khist, vthist=vthist,
                                                kbf=[e[2] for e in kv], vt=[e[3] for e in kv]), wts, n_a + j, j)
        xs = [r[0] for r in res]
    out = []
    for k, st in enumerate(streams):
        bn = xs[k].shape[0]
        k_tail = jnp.concatenate([st["k_hist"].reshape(bn, WIN, KVD), kv[k][0]], axis=1)[:, -WIN:]
        v_tail = jnp.concatenate([st["v_hist"].reshape(bn, WIN, KVD), kv[k][1]], axis=1)[:, -WIN:]
        out.append((xs[k], jnp.stack(tails[k], axis=0), k_tail.reshape(bn, WIN, NKV, HD),
                    v_tail.reshape(bn, WIN, NKV, HD)))
    return out


def kernel(x_prompt, x_sample, c_prompt, c_sample, state_pool, cache_k, cache_v, w_ada, b_ada, g_mix, g_ffn, w_pool, pool_scale, w_q, g_q, sinks, w_o, g_kv, w_ada_kv, b_ada_kv, w_kv, g_k, w_ffn_in, w_ffn_out):
    depth = w_ada.shape[0]
    n_a = state_pool.shape[0]
    bp = x_prompt.shape[0]
    bs, ls, _ = x_sample.shape

    c_all = jnp.concatenate([c_prompt, c_sample, jnp.zeros((16 - bp - bs, D), F32)], axis=0)
    ada = _ada_call(c_all, w_ada, b_ada).reshape(depth, 16, 6, D)
    ada_kv = _ada_call(c_all, w_ada_kv[None], b_ada_kv[None]).reshape(16, 2, D)

    wts = dict(
        g_mix=g_mix.reshape(depth, 1, D), g_ffn=g_ffn.reshape(depth, 1, D),
        w_pool=w_pool, pool_scale=pool_scale.reshape(n_a, 1, D),
        w_q=w_q, g_q=jnp.broadcast_to(g_q[:, :, None], g_q.shape + (LANES,)), sinks=sinks, w_o=w_o,
        g_kv=g_kv.reshape(1, D), w_kv=w_kv, g_k=g_k.reshape(1, HD),
        w_ffn_in=w_ffn_in, w_ffn_out=w_ffn_out,
    )

    zero_kv = jnp.zeros((bp, WIN, NKV, HD), F32)
    prompt = dict(x=x_prompt, mod=ada[:, :bp], modkv=ada_kv[:bp], pool_hist=jnp.zeros((n_a, bp, HALO, D), F32),
                  k_hist=zero_kv, v_hist=zero_kv, ns=1, seq_rows=x_prompt.shape[1], pos0=0)
    sample = dict(x=x_sample, mod=ada[:, bp:bp + bs], modkv=ada_kv[bp:bp + bs],
                  pool_hist=jnp.pad(state_pool, ((0, 0), (0, 0), (HALO - state_pool.shape[2], 0), (0, 0))),
                  k_hist=cache_k, v_hist=cache_v, ns=bs, seq_rows=ls, pos0=PAST_LEN)
    (y_p, pool_p, k_p, v_p), (y_s, pool_s, k_s, v_s) = _trunks([prompt, sample], wts, n_a, depth - n_a)
    return (y_p, y_s, pool_p, k_p, v_p, pool_s, k_s, v_s)
```

```python
import functools

import jax
import jax.numpy as jnp
from jax import lax
from jax.experimental import pallas as pl
from jax.experimental.pallas import tpu as pltpu

D = 1024
DFF = 2816
HD = 64
NH = 16
NKV = 4
GQA = NH // NKV
KVD = NKV * HD
WIN = 128
CHUNK = 64
PAST_LEN = 4096
POOL_WINDOWS = (2, 4, 8, 16)
PGD = D // len(POOL_WINDOWS)
V_AUG_ROWS = 16
HALO = 16
LANES = 128
EPS = 1e-6
NEG_INF = -1e30
BF = jnp.bfloat16
F32 = jnp.float32

V7X_VMEM_LIMIT_BYTES = 60 * 1024 * 1024
TILE_ROWS_POOL = 512
TILE_ROWS_ATTN_KV = 512
TILE_ROWS_ATTN = 512
FFN_COL_CHUNK = 256
ROW_PIECE = 512
ATTN_ROW_PIECE = 512
FFN_OUT_ROWS = 512
PT_SLOTS = 3
WEIGHT_CHUNKS = {"pool": 8, "attn_kv": 16, "attn": 16}
ADA_ROWS = 16
ADA_COL_BLOCKS = (1536, 1024, 128)

NT_DIMS = (((1,), (1,)), ((), ()))
TN_DIMS = (((0,), (0,)), ((), ()))


def _rms(x, g):
    return x * lax.rsqrt(jnp.mean(x * x, axis=-1, keepdims=True) + EPS) * g


def _sigmoid(x):
    return 1.0 / (1.0 + jnp.exp(-x))


def _mod(mod_ref, k, ns, r):
    if ns == 1:
        return mod_ref[0, k:k + 1, :]
    return jnp.concatenate([jnp.broadcast_to(mod_ref[s, k:k + 1, :], (r, D)) for s in range(ns)], axis=0)


def _row_blocks(ns, r):
    tm = ns * r
    rb = min(tm, ATTN_ROW_PIECE)
    assert ns == 1 or rb == tm
    return [slice(b * rb, (b + 1) * rb) for b in range(tm // rb)]


def _load_rows(ref, rows, ns, r):
    return ref[0, rows, :] if ns == 1 else ref[...].reshape(ns * r, ref.shape[-1])


def _store_rows(ref, rows, val, ns, r):
    if ns == 1:
        ref[0, rows, :] = val
    else:
        ref[...] = val.reshape(ns, r, ref.shape[-1])


def _run(pieces):
    for piece in pieces:
        piece()


def _lagged(firsts, seconds, lag):
    out = []
    for k in range(len(firsts) + lag):
        if k < len(firsts):
            out.append(firsts[k])
        if k >= lag:
            out.append(seconds[k - lag])
    return out


def _load_weight_chunks(n, n_chunks, pairs, transposed=()):
    def load(c):
        for src, dst in pairs:
            chunk = src[...].reshape(src.shape[-2:])
            rows = chunk.shape[0]
            if any(dst is ref for ref in transposed):
                dst[:, c * rows:(c + 1) * rows] = chunk.T.astype(BF)
            else:
                dst[c * rows:(c + 1) * rows, :] = chunk.astype(BF)

    for c in range(n_chunks):
        pl.when(n == c)(functools.partial(load, c))


def _ffn_pieces(x1_ref, h2_ref, mod_ref, win_ref, wout_ref, act_ref, o_ref, *, ns, r):
    tm = ns * r
    rb = min(tm, FFN_OUT_ROWS)
    blocks = [slice(b * rb, (b + 1) * rb) for b in range(tm // rb)]

    def chunk(c):
        lo = c * FFN_COL_CHUNK
        hi = lo + FFN_COL_CHUNK
        gate = jnp.dot(h2_ref[...], win_ref[:, lo:hi], preferred_element_type=F32)
        up = jnp.dot(h2_ref[...], win_ref[:, DFF + lo:DFF + hi], preferred_element_type=F32)
        act_ref[:, lo:hi] = (gate * _sigmoid(gate) * up).astype(BF)

    def rows_out(rows):
        y = jnp.dot(act_ref[rows, :], wout_ref[...], preferred_element_type=F32)
        _store_rows(o_ref, rows, x1_ref[rows, :] + _mod(mod_ref, 5, ns, r) * y, ns, r)

    return ([functools.partial(chunk, c) for c in range(DFF // FFN_COL_CHUNK)]
            + [functools.partial(rows_out, rows) for rows in blocks])


def _pool_mixer_pieces(a, t, *, ns, r, pos0, n_tiles):
    rp = min(r, ROW_PIECE)
    x_ref, mod_ref, halo_ref, diff_ref = a["x"], a["mod"], a["halo"], a["diff"]

    def pool(s, p):
        lo = p * rp
        rows = slice(s * r + lo, s * r + lo + rp)

        def m(k):
            return mod_ref[s, k:k + 1, :]

        if p > 0:
            halo = halo_ref[s]
        elif n_tiles == 1:
            halo = a["hist"][s]
        else:
            halo = jnp.where(t == 0, a["hist"][s], halo_ref[s])
        x = x_ref[s, lo:lo + rp, :]
        h = _rms(x, a["gmix"][0] * (1.0 + m(1))) + m(0)
        halo_ref[s] = h[rp - HALO:, :]
        if lo + rp == r:
            a["tail"][s] = h[rp - HALO:, :]

        pos = pos0 + t * r + lo + lax.broadcasted_iota(jnp.int32, (rp, 1), 0)
        for g, w in enumerate(POOL_WINDOWS):
            cols = slice(g * PGD, (g + 1) * PGD)
            acc = jnp.concatenate([halo[:, cols], h[:, cols]], axis=0)
            step = 1
            while step < w:
                acc = acc + pltpu.roll(acc, step, 0)
                step *= 2
            inv_count = 1.0 / jnp.minimum(w, pos + 1).astype(F32)
            diff_ref[rows, cols] = (acc[HALO:, :] * inv_count - h[:, cols]).astype(BF)

    def mix(s, p):
        lo = p * rp
        rows = slice(s * r + lo, s * r + lo + rp)

        def m(k):
            return mod_ref[s, k:k + 1, :]

        mixed = jnp.concatenate(
            [jnp.dot(diff_ref[rows, g * PGD:(g + 1) * PGD], a["wpool_bf"][g], preferred_element_type=F32)
             for g in range(len(POOL_WINDOWS))], axis=1)
        x1 = x_ref[s, lo:lo + rp, :] + (m(2) * a["pscale"][0]) * mixed
        a["x1"][rows, :] = x1
        a["h2"][rows, :] = (_rms(x1, a["gffn"][0] * (1.0 + m(4))) + m(3)).astype(BF)

    sp = [(s, p) for s in range(ns) for p in range(r // rp)]
    return _lagged([functools.partial(pool, *e) for e in sp], [functools.partial(mix, *e) for e in sp], 1)


def _key_mask_features(ns, r, pos0, t, nrows):
    e = lax.broadcasted_iota(jnp.int32, (nrows, LANES), 0)
    f = lax.broadcasted_iota(jnp.int32, (nrows, LANES), 1) - HD
    if ns == 1:
        group = e // CHUNK
        before_start = e < jnp.maximum(0, WIN - (pos0 + t * r))
    else:
        nh = ns * WIN
        is_hist = e < nh
        group = jnp.where(is_hist, e // WIN, (e - nh) // r)
        before_start = is_hist & (e % WIN < WIN - pos0)
    n_groups = _n_key_groups(ns, nrows)
    return ((f == group) | ((f == n_groups) & before_start)).astype(F32).astype(BF)


def _query_mask_bias(ns, r, nrows):
    tm = ns * r
    c = lax.broadcasted_iota(jnp.int32, (HD, tm), 0)
    q = lax.broadcasted_iota(jnp.int32, (HD, tm), 1)
    n_groups = _n_key_groups(ns, nrows)
    if ns == 1:
        d = c - q // CHUNK
        hidden = (d < 0) | (d > WIN // CHUNK)
    else:
        hidden = c != q // r
    hidden = ((c < n_groups) & hidden) | (c == n_groups)
    return jnp.where(hidden, NEG_INF, 0.0).astype(BF)


def _n_key_groups(ns, nrows):
    n_groups = nrows // CHUNK if ns == 1 else ns
    assert n_groups + 1 <= HD
    return n_groups


def _softmax_t(k_blk, qt_blks, qbias, sinks):
    rhs = jnp.concatenate([jnp.concatenate(qt_blks, axis=1), jnp.concatenate([qbias] * GQA, axis=1)], axis=0)
    s = jnp.dot(k_blk, rhs, preferred_element_type=F32)
    ps, sink_terms = [], []
    for g in range(GQA):
        sg = s[:, g * LANES:(g + 1) * LANES]
        sink = jnp.full((1, LANES), sinks[g], F32)
        m = jnp.maximum(jnp.max(sg, axis=0, keepdims=True), sink)
        ps.append(jnp.exp(sg - m).astype(BF))
        sink_terms.append(jnp.exp(sink - m))
    return jnp.concatenate(ps, axis=1), sink_terms


def _attn_mixer_pieces(a, t, *, ns, r, pos0, n_tiles, layer, make_kv):
    tm = ns * r
    nh = ns * WIN
    blocks = _row_blocks(ns, r)
    x_ref, mod_ref, kext_ref, vtext_ref = a["x"], a["mod"], a["kext"], a["vtext"]
    hb_ref, qt_ref, attnt_ref = a["hb"], a["qt"], a["attnt"]

    def v_rows(j):
        return slice(j * (HD + V_AUG_ROWS), j * (HD + V_AUG_ROWS) + HD)

    def setup():
        kprev = a["khist"][...].astype(BF)
        vtprev = a["vthist"][...].astype(BF)
        if n_tiles > 1:
            kprev = jnp.where(t == 0, kprev, kext_ref[:, r:r + WIN, 0:HD])
            vtprev = jnp.where(t == 0, vtprev, jnp.concatenate(
                [vtext_ref[v_rows(j), r:r + WIN] for j in range(NKV)], axis=0))
        feats = _key_mask_features(ns, r, pos0, t, nh + tm)
        ones_row = (lax.broadcasted_iota(jnp.int32, (V_AUG_ROWS, nh + tm), 0) == 0).astype(F32).astype(BF)
        for j in range(NKV):
            kext_ref[j] = feats
            vtext_ref[j * (HD + V_AUG_ROWS) + HD:(j + 1) * (HD + V_AUG_ROWS), :] = ones_row
            vtext_ref[v_rows(j), 0:nh] = vtprev[j * HD:(j + 1) * HD, :]
        a["qbias"][...] = _query_mask_bias(ns, r, nh + tm)
        kext_ref[:, 0:nh, 0:HD] = kprev
        if not make_kv:
            kext_ref[:, nh:, 0:HD] = a["knew"][...]
            for j in range(NKV):
                vtext_ref[v_rows(j), nh:] = a["vtnew"][j * HD:(j + 1) * HD, :]

    def rows_in(rows):
        x = _load_rows(x_ref, rows, ns, r)
        xn = x * lax.rsqrt(jnp.mean(x * x, axis=-1, keepdims=True) + EPS)
        hb_ref[rows, :] = (xn * (a["gmix"][0] * (1.0 + _mod(mod_ref, 1, ns, r))) + _mod(mod_ref, 0, ns, r)).astype(BF)
        if make_kv:
            modkv_ref = a["modkv"]
            a["hkv"][rows, :] = (
                xn * (a["gkv"][...] * (1.0 + _mod(modkv_ref, 1, ns, r))) + _mod(modkv_ref, 0, ns, r)).astype(BF)

    def rows_kv(rows):
        kv = jnp.dot(a["hkv"][rows, :], a["wkv_bf"][...], preferred_element_type=F32)
        v = kv[:, KVD:]
        ks = [_rms(kv[:, j * HD:(j + 1) * HD], a["gk"][...]) for j in range(NKV)]
        ext_rows = slice(nh + rows.start, nh + rows.stop)
        vt = v.T.astype(BF)
        for j in range(NKV):
            kj = ks[j].astype(BF)
            kext_ref[j, ext_rows, 0:HD] = kj
            a["kbf"][j, rows, :] = kj
            vtext_ref[v_rows(j), ext_rows] = vt[j * HD:(j + 1) * HD, :]
        a["vt"][:, rows] = vt
        tr = a["ktail"].shape[1]
        if ns > 1:
            for j in range(NKV):
                a["ktail"][:, :, j * HD:(j + 1) * HD] = ks[j].reshape(ns, r, HD)[:, r - tr:, :]
            a["vtail"][...] = v.reshape(ns, r, KVD)[:, r - tr:, :]
        elif rows.stop == r:
            n = rows.stop - rows.start
            for j in range(NKV):
                a["ktail"][0, :, j * HD:(j + 1) * HD] = ks[j][n - tr:, :]
            a["vtail"][0] = v[n - tr:, :]

    def q_heads(j):
        qt = lax.dot_general(a["wqt_bf"][j * GQA * HD:(j + 1) * GQA * HD, :], hb_ref[...], NT_DIMS,
                             preferred_element_type=F32)
        gq = jnp.concatenate([a["gq"][0] * (HD ** -0.5)] * (tm // LANES), axis=1)
        for g in range(GQA):
            blk = qt[g * HD:(g + 1) * HD, :]
            inv = lax.rsqrt(jnp.mean(blk * blk, axis=0, keepdims=True) + EPS)
            hd = j * GQA + g
            qt_ref[hd * HD:(hd + 1) * HD, :] = (blk * inv * gq).astype(BF)

    sink_terms = {}

    def scores(k, j, q0, k0, nk):
        heads = [j * GQA + g for g in range(GQA)]
        pt, sink_terms[k] = _softmax_t(
            kext_ref[j, k0:k0 + nk, :], [qt_ref[hd * HD:(hd + 1) * HD, q0:q0 + LANES] for hd in heads],
            a["qbias"][:, q0:q0 + LANES], [a["sinks"][layer, hd] for hd in heads])
        a["pt"][k % PT_SLOTS] = pt

    def values(k, j, q0, k0, nk):
        o = jnp.dot(vtext_ref[j * (HD + V_AUG_ROWS):(j + 1) * (HD + V_AUG_ROWS), k0:k0 + nk], a["pt"][k % PT_SLOTS],
                    preferred_element_type=F32)
        for g, sink_term in enumerate(sink_terms.pop(k)):
            hd = j * GQA + g
            lanes = slice(g * LANES, (g + 1) * LANES)
            inv = 1.0 / (o[HD:HD + 1, lanes] + sink_term)
            attnt_ref[hd * HD:(hd + 1) * HD, q0:q0 + LANES] = (o[0:HD, lanes] * inv).astype(BF)

    def rows_mix(rows):
        mix = lax.dot_general(attnt_ref[:, rows], a["wo_bf"][...], TN_DIMS, preferred_element_type=F32)
        x1 = _load_rows(x_ref, rows, ns, r) + _mod(mod_ref, 2, ns, r) * mix
        a["x1"][rows, :] = x1
        a["h2"][rows, :] = (_rms(x1, a["gffn"][0] * (1.0 + _mod(mod_ref, 4, ns, r))) + _mod(mod_ref, 3, ns, r)).astype(BF)

    if ns == 1:
        windows = [(q0, q0, WIN + LANES) for q0 in range(0, tm, LANES)]
    else:
        windows = [(0, 0, nh + tm)]
    pieces = [setup] + [functools.partial(rows_in, rows) for rows in blocks]
    if make_kv:
        pieces += [functools.partial(rows_kv, rows) for rows in blocks]
    pieces += [functools.partial(q_heads, j) for j in range(NKV)]
    att = [(k, j) + w for k, (j, w) in enumerate((j, w) for j in range(NKV) for w in windows)]
    pieces += _lagged([functools.partial(scores, *e) for e in att], [functools.partial(values, *e) for e in att],
                      PT_SLOTS - 1)
    return pieces + [functools.partial(rows_mix, rows) for rows in blocks]


def _layer_kernel(*refs, names, kind, streams, layer):
    refs = dict(zip(names, refs))
    n = pl.program_id(0)
    weights = [(refs["win"], refs["win_bf"]), (refs["wout"], refs["wout_bf"])]
    if kind != "pool":
        weights += [(refs["wq"], refs["wqt_bf"]), (refs["wo"], refs["wo_bf"])]
        if kind == "attn_kv":
            weights.append((refs["wkv"], refs["wkv_bf"]))
    _load_weight_chunks(n, WEIGHT_CHUNKS[kind], weights, transposed=[refs.get("wqt_bf")])
    if kind == "pool":
        @pl.when(n == 0)
        def _():
            refs["wpool_bf"][...] = refs["wpool"][0].astype(BF)

    for sfx, ns, r, pos0, n_tiles, first, n_steps in streams:
        a = dict(refs)
        a.update({k[:-len(sfx)]: v for k, v in refs.items() if sfx and k.endswith(sfx)})

        def tile(a=a, ns=ns, r=r, pos0=pos0, n_tiles=n_tiles, first=first):
            t = lax.rem(n - first, n_tiles)
            if kind == "pool":
                _run(_pool_mixer_pieces(a, t, ns=ns, r=r, pos0=pos0, n_tiles=n_tiles))
            else:
                _run(_attn_mixer_pieces(a, t, ns=ns, r=r, pos0=pos0, n_tiles=n_tiles, layer=layer,
                                        make_kv=kind == "attn_kv"))
            _run(_ffn_pieces(a["x1"], a["h2"], a["mod"], a["win_bf"], a["wout_bf"], a["act"], a["o"], ns=ns, r=r))

        pl.when((n >= first) & (n < first + n_steps))(tile)


def _ada_kernel(c_ref, w_ref, b_ref, o_ref):
    c = c_ref[...]
    c_act = (c * _sigmoid(c)).astype(BF)
    o_ref[0] = jnp.dot(c_act, w_ref[0].astype(BF), preferred_element_type=F32) + b_ref[0]


def _ada_call(c_all, w, b):
    nl, _, n = w.shape
    m = c_all.shape[0]
    nb = next(c for c in ADA_COL_BLOCKS if n % c == 0)
    return pl.pallas_call(
        _ada_kernel,
        grid=(nl, n // nb),
        in_specs=[
            pl.BlockSpec((m, D), lambda l, j: (0, 0)),
            pl.BlockSpec((1, D, nb), lambda l, j: (l, 0, j)),
            pl.BlockSpec((1, 1, nb), lambda l, j: (l, 0, j)),
        ],
        out_specs=pl.BlockSpec((1, m, nb), lambda l, j: (l, 0, j)),
        out_shape=jax.ShapeDtypeStruct((nl, m, n), F32),
        compiler_params=pltpu.CompilerParams(
            dimension_semantics=("arbitrary", "arbitrary"), vmem_limit_bytes=V7X_VMEM_LIMIT_BYTES),
        name="ada",
    )(c_all, w, b.reshape(nl, 1, n))


class _Specs:
    def __init__(self, ns, r, n_tiles, n_steps, first):
        self.ns, self.r, self.n_tiles = ns, r, n_tiles
        self.tile_of = lambda n: jnp.clip(n - first, 0, n_steps - 1)

    def tile(self, width):
        return pl.BlockSpec((self.ns, self.r, width),
                            lambda n: (self.tile_of(n) // self.n_tiles, self.tile_of(n) % self.n_tiles, 0))

    def group(self, rows, width):
        return pl.BlockSpec((self.ns, rows, width), lambda n: (self.tile_of(n) // self.n_tiles, 0, 0))

    def k_rows(self, rows_per_group=None):
        if rows_per_group is None:
            return pl.BlockSpec((NKV, self.ns * self.r, HD), lambda n: (0, self.tile_of(n), 0))
        return pl.BlockSpec((NKV, self.ns * rows_per_group, HD), lambda n: (0, self.tile_of(n) // self.n_tiles, 0))

    def vt_cols(self, rows_per_group=None):
        if rows_per_group is None:
            return pl.BlockSpec((KVD, self.ns * self.r), lambda n: (0, self.tile_of(n)))
        return pl.BlockSpec((KVD, self.ns * rows_per_group), lambda n: (0, self.tile_of(n) // self.n_tiles))

    @staticmethod
    def const(shape):
        return pl.BlockSpec(shape, lambda n: (0,) * len(shape), pipeline_mode=pl.Buffered(1))

    @staticmethod
    def layer(shape, i):
        return pl.BlockSpec((1,) + shape, lambda n: (i,) + (0,) * len(shape), pipeline_mode=pl.Buffered(1))

    @staticmethod
    def weight_chunk(n_chunks, rows, cols, i=None):
        assert rows % n_chunks == 0
        chunk = rows // n_chunks
        if i is None:
            return pl.BlockSpec((chunk, cols), lambda n: (jnp.minimum(n, n_chunks - 1), 0))
        return pl.BlockSpec((1, chunk, cols), lambda n: (i, jnp.minimum(n, n_chunks - 1), 0))


def _layer_call(kind, streams, wts, i, j=None):
    chunked = functools.partial(_Specs.weight_chunk, WEIGHT_CHUNKS[kind])
    ins = [
        ("gmix", wts["g_mix"], _Specs.layer((1, D), i)), ("gffn", wts["g_ffn"], _Specs.layer((1, D), i)),
        ("win", wts["w_ffn_in"], chunked(D, 2 * DFF, i)), ("wout", wts["w_ffn_out"], chunked(DFF, D, i)),
    ]
    outs = []
    scratch = [("win_bf", pltpu.VMEM((D, 2 * DFF), BF)), ("wout_bf", pltpu.VMEM((DFF, D), BF))]
    if kind == "pool":
        ins += [("wpool", wts["w_pool"], _Specs.layer((len(POOL_WINDOWS), PGD, PGD), i)),
                ("pscale", wts["pool_scale"], _Specs.layer((1, D), i))]
        scratch += [("wpool_bf", pltpu.VMEM((len(POOL_WINDOWS), PGD, PGD), BF))]
    else:
        ins += [("sinks", wts["sinks"], pl.BlockSpec(memory_space=pltpu.SMEM)),
                ("gq", wts["g_q"], _Specs.layer((HD, LANES), j)),
                ("wq", wts["w_q"], chunked(D, D, j)), ("wo", wts["w_o"], chunked(D, D, j))]
        scratch += [("wqt_bf", pltpu.VMEM((D, D), BF)), ("wo_bf", pltpu.VMEM((D, D), BF))]
        if kind == "attn_kv":
            ins += [("gkv", wts["g_kv"], _Specs.const((1, D))), ("wkv", wts["w_kv"], chunked(D, 2 * KVD)),
                    ("gk", wts["g_k"], _Specs.const((1, HD)))]
            scratch += [("wkv_bf", pltpu.VMEM((D, 2 * KVD), BF))]

    first = WEIGHT_CHUNKS[kind]
    stream_args, n_outs = [], []
    for k, st in enumerate(streams):
        sfx = "" if k == 0 else "_s%d" % k
        x, ns, r = st["x"], st["ns"], st["r"]
        bn, seq, _ = x.shape
        n_tiles = seq // r
        n_steps = (bn // ns) * n_tiles
        tm = ns * r
        nh = ns * WIN
        tr = min(r, WIN)
        assert ns == 1 or (n_tiles == 1 and r <= CHUNK)
        sp = _Specs(ns, r, n_tiles, n_steps, first)
        s_ins = [("x", x, sp.tile(D)), ("mod", st["mod"], sp.group(6, D))]
        s_outs = [("o", jax.ShapeDtypeStruct((bn, seq, D), F32), sp.tile(D))]
        s_scratch = [("x1", pltpu.VMEM((tm, D), F32)), ("h2", pltpu.VMEM((tm, D), BF)),
                     ("act", pltpu.VMEM((tm, DFF), BF))]
        if kind == "pool":
            s_ins += [("hist", st["pool_hist"], sp.group(HALO, D))]
            s_outs += [("tail", jax.ShapeDtypeStruct((bn, HALO, D), F32), sp.group(HALO, D))]
            s_scratch += [("halo", pltpu.VMEM((ns, HALO, D), F32)), ("diff", pltpu.VMEM((tm, D), BF))]
        else:
            s_ins += [("khist", st["khist"], sp.k_rows(WIN)), ("vthist", st["vthist"], sp.vt_cols(WIN))]
            s_scratch += [
                ("kext", pltpu.VMEM((NKV, nh + tm, 2 * HD), BF)),
                ("vtext", pltpu.VMEM((NKV * (HD + V_AUG_ROWS), nh + tm), BF)), ("qbias", pltpu.VMEM((HD, tm), BF)),
                ("hb", pltpu.VMEM((tm, D), BF)), ("qt", pltpu.VMEM((D, tm), BF)), ("attnt", pltpu.VMEM((D, tm), BF)),
                ("pt", pltpu.VMEM((PT_SLOTS, WIN + LANES if ns == 1 else nh + tm, GQA * LANES), BF)),
            ]
            if kind == "attn_kv":
                s_ins += [("modkv", st["modkv"], sp.group(2, D))]
                s_outs += [
                    ("ktail", jax.ShapeDtypeStruct((bn, tr, KVD), F32), sp.group(tr, KVD)),
                    ("vtail", jax.ShapeDtypeStruct((bn, tr, KVD), F32), sp.group(tr, KVD)),
                    ("kbf", jax.ShapeDtypeStruct((NKV, bn * seq, HD), BF), sp.k_rows()),
                    ("vt", jax.ShapeDtypeStruct((KVD, bn * seq), BF), sp.vt_cols()),
                ]
                s_scratch += [("hkv", pltpu.VMEM((tm, D), BF))]
            else:
                s_ins += [("knew", st["kbf"], sp.k_rows()), ("vtnew", st["vt"], sp.vt_cols())]
        ins += [(e[0] + sfx,) + e[1:] for e in s_ins]
        outs += [(e[0] + sfx,) + e[1:] for e in s_outs]
        scratch += [(e[0] + sfx,) + e[1:] for e in s_scratch]
        stream_args.append((sfx, ns, r, st["pos0"], n_tiles, first, n_steps))
        n_outs.append(len(s_outs))
        first += n_steps

    names = [e[0] for e in ins] + [e[0] for e in outs] + [e[0] for e in scratch]
    flat = pl.pallas_call(
        functools.partial(_layer_kernel, names=names, kind=kind, streams=tuple(stream_args), layer=j),
        grid=(first,),
        in_specs=[e[2] for e in ins],
        out_specs=[e[2] for e in outs],
        out_shape=[e[1] for e in outs],
        scratch_shapes=[e[1] for e in scratch],
        compiler_params=pltpu.CompilerParams(
            dimension_semantics=("arbitrary",), vmem_limit_bytes=V7X_VMEM_LIMIT_BYTES),
        name=kind + "_layer",
    )(*[e[1] for e in ins])
    results, at = [], 0
    for count in n_outs:
        results.append(list(flat[at:at + count]))
        at += count
    return results


def _trunks(streams, wts, n_a, n_b):
    xs = [st["x"] for st in streams]
    tails = [[] for _ in streams]

    def layer_streams(i, tile_rows, **extra):
        return [dict(x=xs[k], mod=st["mod"][i], ns=st["ns"], r=min(st["seq_rows"], tile_rows), pos0=st["pos0"],
                     **{name: vals[k] for name, vals in extra.items()}) for k, st in enumerate(streams)]

    for i in range(n_a):
        res = _layer_call("pool", layer_streams(i, TILE_ROWS_POOL, pool_hist=[st["pool_hist"][i] for st in streams]),
                          wts, i)
        for k, (x, tail) in enumerate(res):
            xs[k] = x
            tails[k].append(tail[:, 1:, :])
    khist = [st["k_hist"].transpose(2, 0, 1, 3).reshape(NKV, -1, HD) for st in streams]
    vthist = [st["v_hist"].transpose(2, 3, 0, 1).reshape(KVD, -1) for st in streams]
    res = _layer_call("attn_kv", layer_streams(n_a, TILE_ROWS_ATTN_KV, khist=khist, vthist=vthist,
                                               modkv=[st["modkv"] for st in streams]), wts, n_a, 0)
    xs = [r[0] for r in res]
    kv = [r[1:] for r in res]
    for j in range(1, n_b):
        res = _layer_call("attn", layer_streams(n_a + j, TILE_ROWS_ATTN, khist=khist, vthist=vthist,
                                                kbf=[e[2] for e in kv], vt=[e[3] for e in kv]), wts, n_a + j, j)
        xs = [r[0] for r in res]
    out = []
    for k, st in enumerate(streams):
        bn = xs[k].shape[0]
        k_tail = jnp.concatenate([st["k_hist"].reshape(bn, WIN, KVD), kv[k][0]], axis=1)[:, -WIN:]
        v_tail = jnp.concatenate([st["v_hist"].reshape(bn, WIN, KVD), kv[k][1]], axis=1)[:, -WIN:]
        out.append((xs[k], jnp.stack(tails[k], axis=0), k_tail.reshape(bn, WIN, NKV, HD),
                    v_tail.reshape(bn, WIN, NKV, HD)))
    return out


def kernel(x_prompt, x_sample, c_prompt, c_sample, state_pool, cache_k, cache_v, w_ada, b_ada, g_mix, g_ffn, w_pool, pool_scale, w_q, g_q, sinks, w_o, g_kv, w_ada_kv, b_ada_kv, w_kv, g_k, w_ffn_in, w_ffn_out):
    depth = w_ada.shape[0]
    n_a = state_pool.shape[0]
    bp = x_prompt.shape[0]
    bs, ls, _ = x_sample.shape

    c_all = jnp.concatenate([c_prompt, c_sample, jnp.zeros((ADA_ROWS - bp - bs, D), F32)], axis=0)
    ada = _ada_call(c_all, w_ada, b_ada).reshape(depth, ADA_ROWS, 6, D)
    ada_kv = _ada_call(c_all, w_ada_kv[None], b_ada_kv[None]).reshape(ADA_ROWS, 2, D)

    wts = dict(
        g_mix=g_mix.reshape(depth, 1, D), g_ffn=g_ffn.reshape(depth, 1, D),
        w_pool=w_pool, pool_scale=pool_scale.reshape(n_a, 1, D),
        w_q=w_q, g_q=jnp.broadcast_to(g_q[:, :, None], g_q.shape + (LANES,)), sinks=sinks, w_o=w_o,
        g_kv=g_kv.reshape(1, D), w_kv=w_kv, g_k=g_k.reshape(1, HD),
        w_ffn_in=w_ffn_in, w_ffn_out=w_ffn_out,
    )

    zero_kv = jnp.zeros((bp, WIN, NKV, HD), F32)
    prompt = dict(x=x_prompt, mod=ada[:, :bp], modkv=ada_kv[:bp], pool_hist=jnp.zeros((n_a, bp, HALO, D), F32),
                  k_hist=zero_kv, v_hist=zero_kv, ns=1, seq_rows=x_prompt.shape[1], pos0=0)
    sample = dict(x=x_sample, mod=ada[:, bp:bp + bs], modkv=ada_kv[bp:bp + bs],
                  pool_hist=jnp.pad(state_pool, ((0, 0), (0, 0), (HALO - state_pool.shape[2], 0), (0, 0))),
                  k_hist=cache_k, v_hist=cache_v, ns=bs, seq_rows=ls, pos0=PAST_LEN)
    (y_p, pool_p, k_p, v_p), (y_s, pool_s, k_s, v_s) = _trunks([prompt, sample], wts, n_a, depth - n_a)
    return (y_p, y_s, pool_p, k_p, v_p, pool_s, k_s, v_s)
```

```python
import functools

import jax
import jax.numpy as jnp
from jax import lax
from jax.experimental import pallas as pl
from jax.experimental.pallas import tpu as pltpu

D = 1024
DFF = 2816
HD = 64
NH = 16
NKV = 4
GQA = NH // NKV
KVD = NKV * HD
WIN = 128
CHUNK = 64
PAST_LEN = 4096
POOL_WINDOWS = (2, 4, 8, 16)
PGD = D // len(POOL_WINDOWS)
V_AUG_ROWS = 16
BAND = 128
HALO = 16
LANES = 128
EPS = 1e-6
NEG_INF = -1e30
LOG2_E = 1.4426950408889634
BF = jnp.bfloat16
F32 = jnp.float32

V7X_VMEM_LIMIT_BYTES = 60 * 1024 * 1024
TILE_ROWS_POOL = 512
TILE_ROWS_ATTN_KV = 512
TILE_ROWS_ATTN = 512
FFN_COL_CHUNK = 256
ROW_PIECE = 512
ATTN_ROW_PIECE = 512
FFN_OUT_ROWS = 512
PT_SLOTS = 3
WEIGHT_CHUNKS = {"pool": 8, "attn_kv": 16, "attn": 16}
ADA_ROWS = 16
ADA_COL_BLOCKS = (1536, 1024, 128)

NT_DIMS = (((1,), (1,)), ((), ()))
TN_DIMS = (((0,), (0,)), ((), ()))


def _rms(x, g):
    return x * lax.rsqrt(jnp.mean(x * x, axis=-1, keepdims=True) + EPS) * g


def _sigmoid(x):
    return 1.0 / (1.0 + jnp.exp(-x))


def _mod(mod_ref, k, ns, r):
    if ns == 1:
        return mod_ref[0, k:k + 1, :]
    return jnp.concatenate([jnp.broadcast_to(mod_ref[s, k:k + 1, :], (r, D)) for s in range(ns)], axis=0)


def _row_blocks(ns, r):
    tm = ns * r
    rb = min(tm, ATTN_ROW_PIECE)
    assert ns == 1 or rb == tm
    return [slice(b * rb, (b + 1) * rb) for b in range(tm // rb)]


def _load_rows(ref, rows, ns, r):
    return ref[0, rows, :] if ns == 1 else ref[...].reshape(ns * r, ref.shape[-1])


def _store_rows(ref, rows, val, ns, r):
    if ns == 1:
        ref[0, rows, :] = val
    else:
        ref[...] = val.reshape(ns, r, ref.shape[-1])


def _run(pieces):
    for piece in pieces:
        piece()


def _lagged(firsts, seconds, lag):
    out = []
    for k in range(len(firsts) + lag):
        if k < len(firsts):
            out.append(firsts[k])
        if k >= lag:
            out.append(seconds[k - lag])
    return out


def _load_weight_chunks(n, n_chunks, pairs, transposed=()):
    def load(c):
        for src, dst in pairs:
            chunk = src[...].reshape(src.shape[-2:])
            rows = chunk.shape[0]
            if any(dst is ref for ref in transposed):
                dst[:, c * rows:(c + 1) * rows] = chunk.T.astype(BF)
            else:
                dst[c * rows:(c + 1) * rows, :] = chunk.astype(BF)

    for c in range(n_chunks):
        pl.when(n == c)(functools.partial(load, c))


def _ffn_pieces(x1_ref, h2_ref, mod_ref, win_ref, wout_ref, act_ref, o_ref, *, ns, r):
    tm = ns * r
    rb = min(tm, FFN_OUT_ROWS)
    blocks = [slice(b * rb, (b + 1) * rb) for b in range(tm // rb)]

    def chunk(c):
        lo = c * FFN_COL_CHUNK
        hi = lo + FFN_COL_CHUNK
        gate = jnp.dot(h2_ref[...], win_ref[:, lo:hi], preferred_element_type=F32)
        up = jnp.dot(h2_ref[...], win_ref[:, DFF + lo:DFF + hi], preferred_element_type=F32)
        act_ref[:, lo:hi] = (gate * _sigmoid(gate) * up).astype(BF)

    def rows_out(rows):
        y = jnp.dot(act_ref[rows, :], wout_ref[...], preferred_element_type=F32)
        _store_rows(o_ref, rows, x1_ref[rows, :] + _mod(mod_ref, 5, ns, r) * y, ns, r)

    return ([functools.partial(chunk, c) for c in range(DFF // FFN_COL_CHUNK)]
            + [functools.partial(rows_out, rows) for rows in blocks])


def _pool_mixer_pieces(a, t, *, ns, r, pos0, n_tiles):
    rp = min(r, ROW_PIECE)
    x_ref, mod_ref, halo_ref, diff_ref = a["x"], a["mod"], a["halo"], a["diff"]

    def pool(s, p):
        lo = p * rp
        rows = slice(s * r + lo, s * r + lo + rp)

        def m(k):
            return mod_ref[s, k:k + 1, :]

        if p > 0:
            halo = halo_ref[s]
        elif n_tiles == 1:
            halo = a["hist"][s]
        else:
            halo = jnp.where(t == 0, a["hist"][s], halo_ref[s])
        x = x_ref[s, lo:lo + rp, :]
        h = _rms(x, a["gmix"][0] * (1.0 + m(1))) + m(0)
        halo_ref[s] = h[rp - HALO:, :]
        if lo + rp == r:
            a["tail"][s] = h[rp - HALO:, :]

        pos = pos0 + t * r + lo + lax.broadcasted_iota(jnp.int32, (rp, 1), 0)
        if r % BAND == 0:
            pext_ref = a["pext"]
            if p == 0:
                prev = jnp.concatenate([jnp.zeros((BAND - HALO, D), BF), halo.astype(BF)], axis=0)
                if n_tiles > 1:
                    prev = jnp.where(t == 0, prev, pext_ref[s, r:r + BAND, :])
                pext_ref[s, 0:BAND, :] = prev
            pext_ref[s, BAND + lo:BAND + lo + rp, :] = h.astype(BF)
            for sb in range(rp // BAND):
                sub = slice(sb * BAND, (sb + 1) * BAND)
                for g, w in enumerate(POOL_WINDOWS):
                    cols = slice(g * PGD, (g + 1) * PGD)
                    sums = jnp.dot(a["band"][g], pext_ref[s, lo + sb * BAND:lo + (sb + 2) * BAND, cols],
                                   preferred_element_type=F32)
                    inv_count = 1.0 / jnp.minimum(w, pos[sub] + 1).astype(F32)
                    diff_ref[s * r + lo + sb * BAND:s * r + lo + (sb + 1) * BAND, cols] = (
                        sums * inv_count - h[sub, cols]).astype(BF)
            return
        for g, w in enumerate(POOL_WINDOWS):
            cols = slice(g * PGD, (g + 1) * PGD)
            acc = jnp.concatenate([halo[:, cols], h[:, cols]], axis=0)
            step = 1
            while step < w:
                acc = acc + pltpu.roll(acc, step, 0)
                step *= 2
            inv_count = 1.0 / jnp.minimum(w, pos + 1).astype(F32)
            diff_ref[rows, cols] = (acc[HALO:, :] * inv_count - h[:, cols]).astype(BF)

    def mix(s, p):
        lo = p * rp
        rows = slice(s * r + lo, s * r + lo + rp)

        def m(k):
            return mod_ref[s, k:k + 1, :]

        mixed = jnp.concatenate(
            [jnp.dot(diff_ref[rows, g * PGD:(g + 1) * PGD], a["wpool_bf"][g], preferred_element_type=F32)
             for g in range(len(POOL_WINDOWS))], axis=1)
        x1 = x_ref[s, lo:lo + rp, :] + (m(2) * a["pscale"][0]) * mixed
        a["x1"][rows, :] = x1
        a["h2"][rows, :] = (_rms(x1, a["gffn"][0] * (1.0 + m(4))) + m(3)).astype(BF)

    sp = [(s, p) for s in range(ns) for p in range(r // rp)]
    return _lagged([functools.partial(pool, *e) for e in sp], [functools.partial(mix, *e) for e in sp], 1)


def _key_mask_features(ns, r, pos0, t, nrows):
    e = lax.broadcasted_iota(jnp.int32, (nrows, LANES), 0)
    f = lax.broadcasted_iota(jnp.int32, (nrows, LANES), 1) - HD
    if ns == 1:
        group = e // CHUNK
        before_start = e < jnp.maximum(0, WIN - (pos0 + t * r))
    else:
        nh = ns * WIN
        is_hist = e < nh
        group = jnp.where(is_hist, e // WIN, (e - nh) // r)
        before_start = is_hist & (e % WIN < WIN - pos0)
    n_groups = _n_key_groups(ns, nrows)
    return ((f == group) | ((f == n_groups) & before_start)).astype(F32).astype(BF)


def _query_mask_bias(ns, r, nrows):
    tm = ns * r
    c = lax.broadcasted_iota(jnp.int32, (HD, tm), 0)
    q = lax.broadcasted_iota(jnp.int32, (HD, tm), 1)
    n_groups = _n_key_groups(ns, nrows)
    if ns == 1:
        d = c - q // CHUNK
        hidden = (d < 0) | (d > WIN // CHUNK)
    else:
        hidden = c != q // r
    hidden = ((c < n_groups) & hidden) | (c == n_groups)
    return jnp.where(hidden, NEG_INF, 0.0).astype(BF)


def _n_key_groups(ns, nrows):
    n_groups = nrows // CHUNK if ns == 1 else ns
    assert n_groups + 1 <= HD
    return n_groups


def _softmax_t(k_blk, qt_blks, qbias, sinks):
    rhs = jnp.concatenate([jnp.concatenate(qt_blks, axis=1), jnp.concatenate([qbias] * GQA, axis=1)], axis=0)
    s = jnp.dot(k_blk, rhs, preferred_element_type=F32)
    ps, sink_terms = [], []
    for g in range(GQA):
        sg = s[:, g * LANES:(g + 1) * LANES]
        sink = jnp.full((1, LANES), sinks[g] * LOG2_E, F32)
        m = jnp.maximum(jnp.max(sg, axis=0, keepdims=True), sink)
        ps.append(jnp.exp2(sg - m).astype(BF))
        sink_terms.append(jnp.exp2(sink - m))
    return jnp.concatenate(ps, axis=1), sink_terms


def _attn_mixer_pieces(a, t, *, ns, r, pos0, n_tiles, layer, make_kv):
    tm = ns * r
    nh = ns * WIN
    blocks = _row_blocks(ns, r)
    x_ref, mod_ref, kext_ref, vtext_ref = a["x"], a["mod"], a["kext"], a["vtext"]
    hb_ref, qt_ref, attnt_ref = a["hb"], a["qt"], a["attnt"]

    def v_rows(j):
        return slice(j * (HD + V_AUG_ROWS), j * (HD + V_AUG_ROWS) + HD)

    def setup():
        kprev = a["khist"][...].astype(BF)
        vtprev = a["vthist"][...].astype(BF)
        if n_tiles > 1:
            kprev = jnp.where(t == 0, kprev, kext_ref[:, r:r + WIN, 0:HD])
            vtprev = jnp.where(t == 0, vtprev, jnp.concatenate(
                [vtext_ref[v_rows(j), r:r + WIN] for j in range(NKV)], axis=0))
        feats = _key_mask_features(ns, r, pos0, t, nh + tm)
        ones_row = (lax.broadcasted_iota(jnp.int32, (V_AUG_ROWS, nh + tm), 0) == 0).astype(F32).astype(BF)
        for j in range(NKV):
            kext_ref[j] = feats
            vtext_ref[j * (HD + V_AUG_ROWS) + HD:(j + 1) * (HD + V_AUG_ROWS), :] = ones_row
            vtext_ref[v_rows(j), 0:nh] = vtprev[j * HD:(j + 1) * HD, :]
        a["qbias"][...] = _query_mask_bias(ns, r, nh + tm)
        kext_ref[:, 0:nh, 0:HD] = kprev
        if not make_kv:
            kext_ref[:, nh:, 0:HD] = a["knew"][...]
            for j in range(NKV):
                vtext_ref[v_rows(j), nh:] = a["vtnew"][j * HD:(j + 1) * HD, :]

    def rows_in(rows):
        x = _load_rows(x_ref, rows, ns, r)
        xn = x * lax.rsqrt(jnp.mean(x * x, axis=-1, keepdims=True) + EPS)
        hb_ref[rows, :] = (xn * (a["gmix"][0] * (1.0 + _mod(mod_ref, 1, ns, r))) + _mod(mod_ref, 0, ns, r)).astype(BF)
        if make_kv:
            modkv_ref = a["modkv"]
            a["hkv"][rows, :] = (
                xn * (a["gkv"][...] * (1.0 + _mod(modkv_ref, 1, ns, r))) + _mod(modkv_ref, 0, ns, r)).astype(BF)

    def rows_kv(rows):
        kv = jnp.dot(a["hkv"][rows, :], a["wkv_bf"][...], preferred_element_type=F32)
        v = kv[:, KVD:]
        ks = [_rms(kv[:, j * HD:(j + 1) * HD], a["gk"][...]) for j in range(NKV)]
        ext_rows = slice(nh + rows.start, nh + rows.stop)
        vt = v.T.astype(BF)
        for j in range(NKV):
            kj = ks[j].astype(BF)
            kext_ref[j, ext_rows, 0:HD] = kj
            a["kbf"][j, rows, :] = kj
            vtext_ref[v_rows(j), ext_rows] = vt[j * HD:(j + 1) * HD, :]
        a["vt"][:, rows] = vt
        tr = a["ktail"].shape[1]
        if ns > 1:
            for j in range(NKV):
                a["ktail"][:, :, j * HD:(j + 1) * HD] = ks[j].reshape(ns, r, HD)[:, r - tr:, :]
            a["vtail"][...] = v.reshape(ns, r, KVD)[:, r - tr:, :]
        elif rows.stop == r:
            n = rows.stop - rows.start
            for j in range(NKV):
                a["ktail"][0, :, j * HD:(j + 1) * HD] = ks[j][n - tr:, :]
            a["vtail"][0] = v[n - tr:, :]

    def q_heads(j):
        qt = lax.dot_general(a["wqt_bf"][j * GQA * HD:(j + 1) * GQA * HD, :], hb_ref[...], NT_DIMS,
                             preferred_element_type=F32)
        gq = jnp.concatenate([a["gq"][0] * (HD ** -0.5 * LOG2_E)] * (tm // LANES), axis=1)
        for g in range(GQA):
            blk = qt[g * HD:(g + 1) * HD, :]
            inv = lax.rsqrt(jnp.mean(blk * blk, axis=0, keepdims=True) + EPS)
            hd = j * GQA + g
            qt_ref[hd * HD:(hd + 1) * HD, :] = (blk * inv * gq).astype(BF)

    sink_terms = {}

    def scores(k, j, q0, k0, nk):
        heads = [j * GQA + g for g in range(GQA)]
        pt, sink_terms[k] = _softmax_t(
            kext_ref[j, k0:k0 + nk, :], [qt_ref[hd * HD:(hd + 1) * HD, q0:q0 + LANES] for hd in heads],
            a["qbias"][:, q0:q0 + LANES], [a["sinks"][layer, hd] for hd in heads])
        a["pt"][k % PT_SLOTS] = pt

    def values(k, j, q0, k0, nk):
        o = jnp.dot(vtext_ref[j * (HD + V_AUG_ROWS):(j + 1) * (HD + V_AUG_ROWS), k0:k0 + nk], a["pt"][k % PT_SLOTS],
                    preferred_element_type=F32)
        for g, sink_term in enumerate(sink_terms.pop(k)):
            hd = j * GQA + g
            lanes = slice(g * LANES, (g + 1) * LANES)
            inv = 1.0 / (o[HD:HD + 1, lanes] + sink_term)
            attnt_ref[hd * HD:(hd + 1) * HD, q0:q0 + LANES] = (o[0:HD, lanes] * inv).astype(BF)

    def rows_mix(rows):
        mix = lax.dot_general(attnt_ref[:, rows], a["wo_bf"][...], TN_DIMS, preferred_element_type=F32)
        x1 = _load_rows(x_ref, rows, ns, r) + _mod(mod_ref, 2, ns, r) * mix
        a["x1"][rows, :] = x1
        a["h2"][rows, :] = (_rms(x1, a["gffn"][0] * (1.0 + _mod(mod_ref, 4, ns, r))) + _mod(mod_ref, 3, ns, r)).astype(BF)

    if ns == 1:
        windows = [(q0, q0, WIN + LANES) for q0 in range(0, tm, LANES)]
    else:
        windows = [(0, 0, nh + tm)]
    pieces = [setup] + [functools.partial(rows_in, rows) for rows in blocks]
    if make_kv:
        pieces += [functools.partial(rows_kv, rows) for rows in blocks]
    pieces += [functools.partial(q_heads, j) for j in range(NKV)]
    att = [(k, j) + w for k, (j, w) in enumerate((j, w) for j in range(NKV) for w in windows)]
    pieces += _lagged([functools.partial(scores, *e) for e in att], [functools.partial(values, *e) for e in att],
                      PT_SLOTS - 1)
    return pieces + [functools.partial(rows_mix, rows) for rows in blocks]


def _layer_kernel(*refs, names, kind, streams, layer):
    refs = dict(zip(names, refs))
    n = pl.program_id(0)
    weights = [(refs["win"], refs["win_bf"]), (refs["wout"], refs["wout_bf"])]
    if kind != "pool":
        weights += [(refs["wq"], refs["wqt_bf"]), (refs["wo"], refs["wo_bf"])]
        if kind == "attn_kv":
            weights.append((refs["wkv"], refs["wkv_bf"]))
    _load_weight_chunks(n, WEIGHT_CHUNKS[kind], weights, transposed=[refs.get("wqt_bf")])
    if kind == "pool":
        @pl.when(n == 0)
        def _():
            refs["wpool_bf"][...] = refs["wpool"][0].astype(BF)

    for sfx, ns, r, pos0, n_tiles, first, n_steps in streams:
        a = dict(refs)
        a.update({k[:-len(sfx)]: v for k, v in refs.items() if sfx and k.endswith(sfx)})

        def tile(a=a, ns=ns, r=r, pos0=pos0, n_tiles=n_tiles, first=first):
            t = lax.rem(n - first, n_tiles)
            if kind == "pool":
                _run(_pool_mixer_pieces(a, t, ns=ns, r=r, pos0=pos0, n_tiles=n_tiles))
            else:
                _run(_attn_mixer_pieces(a, t, ns=ns, r=r, pos0=pos0, n_tiles=n_tiles, layer=layer,
                                        make_kv=kind == "attn_kv"))
            _run(_ffn_pieces(a["x1"], a["h2"], a["mod"], a["win_bf"], a["wout_bf"], a["act"], a["o"], ns=ns, r=r))

        pl.when((n >= first) & (n < first + n_steps))(tile)


def _ada_kernel(c_ref, w_ref, b_ref, o_ref):
    c = c_ref[...]
    c_act = (c * _sigmoid(c)).astype(BF)
    o_ref[0] = jnp.dot(c_act, w_ref[0].astype(BF), preferred_element_type=F32) + b_ref[0]


def _ada_call(c_all, w, b):
    nl, _, n = w.shape
    m = c_all.shape[0]
    nb = next(c for c in ADA_COL_BLOCKS if n % c == 0)
    return pl.pallas_call(
        _ada_kernel,
        grid=(nl, n // nb),
        in_specs=[
            pl.BlockSpec((m, D), lambda l, j: (0, 0)),
            pl.BlockSpec((1, D, nb), lambda l, j: (l, 0, j)),
            pl.BlockSpec((1, 1, nb), lambda l, j: (l, 0, j)),
        ],
        out_specs=pl.BlockSpec((1, m, nb), lambda l, j: (l, 0, j)),
        out_shape=jax.ShapeDtypeStruct((nl, m, n), F32),
        compiler_params=pltpu.CompilerParams(
            dimension_semantics=("arbitrary", "arbitrary"), vmem_limit_bytes=V7X_VMEM_LIMIT_BYTES),
        name="ada",
    )(c_all, w, b.reshape(nl, 1, n))


class _Specs:
    def __init__(self, ns, r, n_tiles, n_steps, first):
        self.ns, self.r, self.n_tiles = ns, r, n_tiles
        self.tile_of = lambda n: jnp.clip(n - first, 0, n_steps - 1)

    def tile(self, width):
        return pl.BlockSpec((self.ns, self.r, width),
                            lambda n: (self.tile_of(n) // self.n_tiles, self.tile_of(n) % self.n_tiles, 0))

    def group(self, rows, width):
        return pl.BlockSpec((self.ns, rows, width), lambda n: (self.tile_of(n) // self.n_tiles, 0, 0))

    def k_rows(self, rows_per_group=None):
        if rows_per_group is None:
            return pl.BlockSpec((NKV, self.ns * self.r, HD), lambda n: (0, self.tile_of(n), 0))
        return pl.BlockSpec((NKV, self.ns * rows_per_group, HD), lambda n: (0, self.tile_of(n) // self.n_tiles, 0))

    def vt_cols(self, rows_per_group=None):
        if rows_per_group is None:
            return pl.BlockSpec((KVD, self.ns * self.r), lambda n: (0, self.tile_of(n)))
        return pl.BlockSpec((KVD, self.ns * rows_per_group), lambda n: (0, self.tile_of(n) // self.n_tiles))

    @staticmethod
    def const(shape):
        return pl.BlockSpec(shape, lambda n: (0,) * len(shape), pipeline_mode=pl.Buffered(1))

    @staticmethod
    def layer(shape, i):
        return pl.BlockSpec((1,) + shape, lambda n: (i,) + (0,) * len(shape), pipeline_mode=pl.Buffered(1))

    @staticmethod
    def weight_chunk(n_chunks, rows, cols, i=None):
        assert rows % n_chunks == 0
        chunk = rows // n_chunks
        if i is None:
            return pl.BlockSpec((chunk, cols), lambda n: (jnp.minimum(n, n_chunks - 1), 0))
        return pl.BlockSpec((1, chunk, cols), lambda n: (i, jnp.minimum(n, n_chunks - 1), 0))


def _pool_band_matrices():
    i = lax.broadcasted_iota(jnp.int32, (BAND, 2 * BAND), 0)
    j = lax.broadcasted_iota(jnp.int32, (BAND, 2 * BAND), 1)
    d = BAND + i - j
    return jnp.stack([((d >= 0) & (d < w)).astype(BF) for w in POOL_WINDOWS], axis=0)


def _layer_call(kind, streams, wts, i, j=None):
    chunked = functools.partial(_Specs.weight_chunk, WEIGHT_CHUNKS[kind])
    ins = [
        ("gmix", wts["g_mix"], _Specs.layer((1, D), i)), ("gffn", wts["g_ffn"], _Specs.layer((1, D), i)),
        ("win", wts["w_ffn_in"], chunked(D, 2 * DFF, i)), ("wout", wts["w_ffn_out"], chunked(DFF, D, i)),
    ]
    outs = []
    scratch = [("win_bf", pltpu.VMEM((D, 2 * DFF), BF)), ("wout_bf", pltpu.VMEM((DFF, D), BF))]
    if kind == "pool":
        ins += [("wpool", wts["w_pool"], _Specs.layer((len(POOL_WINDOWS), PGD, PGD), i)),
                ("pscale", wts["pool_scale"], _Specs.layer((1, D), i)),
                ("band", _pool_band_matrices(), _Specs.const((len(POOL_WINDOWS), BAND, 2 * BAND)))]
        scratch += [("wpool_bf", pltpu.VMEM((len(POOL_WINDOWS), PGD, PGD), BF))]
    else:
        ins += [("sinks", wts["sinks"], pl.BlockSpec(memory_space=pltpu.SMEM)),
                ("gq", wts["g_q"], _Specs.layer((HD, LANES), j)),
                ("wq", wts["w_q"], chunked(D, D, j)), ("wo", wts["w_o"], chunked(D, D, j))]
        scratch += [("wqt_bf", pltpu.VMEM((D, D), BF)), ("wo_bf", pltpu.VMEM((D, D), BF))]
        if kind == "attn_kv":
            ins += [("gkv", wts["g_kv"], _Specs.const((1, D))), ("wkv", wts["w_kv"], chunked(D, 2 * KVD)),
                    ("gk", wts["g_k"], _Specs.const((1, HD)))]
            scratch += [("wkv_bf", pltpu.VMEM((D, 2 * KVD), BF))]

    first = WEIGHT_CHUNKS[kind]
    stream_args, n_outs = [], []
    for k, st in enumerate(streams):
        sfx = "" if k == 0 else "_s%d" % k
        x, ns, r = st["x"], st["ns"], st["r"]
        bn, seq, _ = x.shape
        n_tiles = seq // r
        n_steps = (bn // ns) * n_tiles
        tm = ns * r
        nh = ns * WIN
        tr = min(r, WIN)
        assert ns == 1 or (n_tiles == 1 and r <= CHUNK)
        sp = _Specs(ns, r, n_tiles, n_steps, first)
        s_ins = [("x", x, sp.tile(D)), ("mod", st["mod"], sp.group(6, D))]
        s_outs = [("o", jax.ShapeDtypeStruct((bn, seq, D), F32), sp.tile(D))]
        s_scratch = [("x1", pltpu.VMEM((tm, D), F32)), ("h2", pltpu.VMEM((tm, D), BF)),
                     ("act", pltpu.VMEM((tm, DFF), BF))]
        if kind == "pool":
            s_ins += [("hist", st["pool_hist"], sp.group(HALO, D))]
            s_outs += [("tail", jax.ShapeDtypeStruct((bn, HALO, D), F32), sp.group(HALO, D))]
            s_scratch += [("halo", pltpu.VMEM((ns, HALO, D), F32)), ("diff", pltpu.VMEM((tm, D), BF))]
            if r % BAND == 0:
                s_scratch += [("pext", pltpu.VMEM((ns, r + BAND, D), BF))]
        else:
            s_ins += [("khist", st["khist"], sp.k_rows(WIN)), ("vthist", st["vthist"], sp.vt_cols(WIN))]
            s_scratch += [
                ("kext", pltpu.VMEM((NKV, nh + tm, 2 * HD), BF)),
                ("vtext", pltpu.VMEM((NKV * (HD + V_AUG_ROWS), nh + tm), BF)), ("qbias", pltpu.VMEM((HD, tm), BF)),
                ("hb", pltpu.VMEM((tm, D), BF)), ("qt", pltpu.VMEM((D, tm), BF)), ("attnt", pltpu.VMEM((D, tm), BF)),
                ("pt", pltpu.VMEM((PT_SLOTS, WIN + LANES if ns == 1 else nh + tm, GQA * LANES), BF)),
            ]
            if kind == "attn_kv":
                s_ins += [("modkv", st["modkv"], sp.group(2, D))]
                s_outs += [
                    ("ktail", jax.ShapeDtypeStruct((bn, tr, KVD), F32), sp.group(tr, KVD)),
                    ("vtail", jax.ShapeDtypeStruct((bn, tr, KVD), F32), sp.group(tr, KVD)),
                    ("kbf", jax.ShapeDtypeStruct((NKV, bn * seq, HD), BF), sp.k_rows()),
                    ("vt", jax.ShapeDtypeStruct((KVD, bn * seq), BF), sp.vt_cols()),
                ]
                s_scratch += [("hkv", pltpu.VMEM((tm, D), BF))]
            else:
                s_ins += [("knew", st["kbf"], sp.k_rows()), ("vtnew", st["vt"], sp.vt_cols())]
        ins += [(e[0] + sfx,) + e[1:] for e in s_ins]
        outs += [(e[0] + sfx,) + e[1:] for e in s_outs]
        scratch += [(e[0] + sfx,) + e[1:] for e in s_scratch]
        stream_args.append((sfx, ns, r, st["pos0"], n_tiles, first, n_steps))
        n_outs.append(len(s_outs))
        first += n_steps

    names = [e[0] for e in ins] + [e[0] for e in outs] + [e[0] for e in scratch]
    flat = pl.pallas_call(
        functools.partial(_layer_kernel, names=names, kind=kind, streams=tuple(stream_args), layer=j),
        grid=(first,),
        in_specs=[e[2] for e in ins],
        out_specs=[e[2] for e in outs],
        out_shape=[e[1] for e in outs],
        scratch_shapes=[e[1] for e in scratch],
        compiler_params=pltpu.CompilerParams(
            dimension_semantics=("arbitrary",), vmem_limit_bytes=V7X_VMEM_LIMIT_BYTES),
        name=kind + "_layer",
    )(*[e[1] for e in ins])
    results, at = [], 0
    for count in n_outs:
        results.append(list(flat[at:at + count]))
        at += count
    return results


def _trunks(streams, wts, n_a, n_b):
    xs = [st["x"] for st in streams]
    tails = [[] for _ in streams]

    def layer_streams(i, tile_rows, **extra):
        return [dict(x=xs[k], mod=st["mod"][i], ns=st["ns"], r=min(st["seq_rows"], tile_rows), pos0=st["pos0"],
                     **{name: vals[k] for name, vals in extra.items()}) for k, st in enumerate(streams)]

    for i in range(n_a):
        res = _layer_call("pool", layer_streams(i, TILE_ROWS_POOL, pool_hist=[st["pool_hist"][i] for st in streams]),
                          wts, i)
        for k, (x, tail) in enumerate(res):
            xs[k] = x
            tails[k].append(tail[:, 1:, :])
    khist = [st["k_hist"].transpose(2, 0, 1, 3).reshape(NKV, -1, HD) for st in streams]
    vthist = [st["v_hist"].transpose(2, 3, 0, 1).reshape(KVD, -1) for st in streams]
    res = _layer_call("attn_kv", layer_streams(n_a, TILE_ROWS_ATTN_KV, khist=khist, vthist=vthist,
                                               modkv=[st["modkv"] for st in streams]), wts, n_a, 0)
    xs = [r[0] for r in res]
    kv = [r[1:] for r in res]
    for j in range(1, n_b):
        res = _layer_call("attn", layer_streams(n_a + j, TILE_ROWS_ATTN, khist=khist, vthist=vthist,
                                                kbf=[e[2] for e in kv], vt=[e[3] for e in kv]), wts, n_a + j, j)
        xs = [r[0] for r in res]
    out = []
    for k, st in enumerate(streams):
        bn = xs[k].shape[0]
        k_tail = jnp.concatenate([st["k_hist"].reshape(bn, WIN, KVD), kv[k][0]], axis=1)[:, -WIN:]
        v_tail = jnp.concatenate([st["v_hist"].reshape(bn, WIN, KVD), kv[k][1]], axis=1)[:, -WIN:]
        out.append((xs[k], jnp.stack(tails[k], axis=0), k_tail.reshape(bn, WIN, NKV, HD),
                    v_tail.reshape(bn, WIN, NKV, HD)))
    return out


def kernel(x_prompt, x_sample, c_prompt, c_sample, state_pool, cache_k, cache_v, w_ada, b_ada, g_mix, g_ffn, w_pool, pool_scale, w_q, g_q, sinks, w_o, g_kv, w_ada_kv, b_ada_kv, w_kv, g_k, w_ffn_in, w_ffn_out):
    depth = w_ada.shape[0]
    n_a = state_pool.shape[0]
    bp = x_prompt.shape[0]
    bs, ls, _ = x_sample.shape

    c_all = jnp.concatenate([c_prompt, c_sample, jnp.zeros((ADA_ROWS - bp - bs, D), F32)], axis=0)
    ada = _ada_call(c_all, w_ada, b_ada).reshape(depth, ADA_ROWS, 6, D)
    ada_kv = _ada_call(c_all, w_ada_kv[None], b_ada_kv[None]).reshape(ADA_ROWS, 2, D)

    wts = dict(
        g_mix=g_mix.reshape(depth, 1, D), g_ffn=g_ffn.reshape(depth, 1, D),
        w_pool=w_pool, pool_scale=pool_scale.reshape(n_a, 1, D),
        w_q=w_q, g_q=jnp.broadcast_to(g_q[:, :, None], g_q.shape + (LANES,)), sinks=sinks, w_o=w_o,
        g_kv=g_kv.reshape(1, D), w_kv=w_kv, g_k=g_k.reshape(1, HD),
        w_ffn_in=w_ffn_in, w_ffn_out=w_ffn_out,
    )

    zero_kv = jnp.zeros((bp, WIN, NKV, HD), F32)
    prompt = dict(x=x_prompt, mod=ada[:, :bp], modkv=ada_kv[:bp], pool_hist=jnp.zeros((n_a, bp, HALO, D), F32),
                  k_hist=zero_kv, v_hist=zero_kv, ns=1, seq_rows=x_prompt.shape[1], pos0=0)
    sample = dict(x=x_sample, mod=ada[:, bp:bp + bs], modkv=ada_kv[bp:bp + bs],
                  pool_hist=jnp.pad(state_pool, ((0, 0), (0, 0), (HALO - state_pool.shape[2], 0), (0, 0))),
                  k_hist=cache_k, v_hist=cache_v, ns=bs, seq_rows=ls, pos0=PAST_LEN)
    (y_p, pool_p, k_p, v_p), (y_s, pool_s, k_s, v_s) = _trunks([prompt, sample], wts, n_a, depth - n_a)
    return (y_p, y_s, pool_p, k_p, v_p, pool_s, k_s, v_s)
```
